```python
import jax, jax.numpy as jnp
from jax import lax
import numpy as np

D_MODEL = 4096
BATCH = 4
SEQ = 4096
DEPTH = 4

CTX_LEN = 256
GRID_W = 64
HEAD_DIM = 128
NA_HEADS = 16
NA_KH = 8
NA_KW = 16
NA_QCB = 16
WA_HEADS = 16
WA_KV_HEADS = 4
WA_WINDOW = 128
WA_BLOCK = 128
FNET_GROUPS = 16
FNET_GROUP_DIM = 128
FNET_WIDTH = FNET_GROUPS * FNET_GROUP_DIM
SC_WIDTH = 2048
NA_WIDTH = NA_HEADS * HEAD_DIM
WA_Q_WIDTH = WA_HEADS * HEAD_DIM
WA_KV_WIDTH = WA_KV_HEADS * HEAD_DIM
MIX_WIDTH = NA_WIDTH + WA_Q_WIDTH
EVEN_PROJ = 3 * NA_WIDTH + WA_Q_WIDTH + 2 * WA_KV_WIDTH
ODD_PROJ = FNET_WIDTH + 3 * SC_WIDTH
EVEN_SPLITS = [NA_WIDTH, 2 * NA_WIDTH, 3 * NA_WIDTH, 3 * NA_WIDTH + WA_Q_WIDTH,
               3 * NA_WIDTH + WA_Q_WIDTH + WA_KV_WIDTH]
ODD_SPLITS = [FNET_WIDTH, FNET_WIDTH + SC_WIDTH, FNET_WIDTH + 2 * SC_WIDTH]
FFN_HIDDEN = 5632
CONV_W = 3
ROPE_THETA = 10000.0
RMS_EPS = 1e-6
NEG_INF = -1e30
MOD_INIT = 0.5
N_EVEN = (DEPTH + 1) // 2
N_ODD = DEPTH // 2

kernel_name = "hybrid_na_wgqa_fnet_shortconv_dit_trunk"


def rmsnorm(x, g):
    xf = x.astype(jnp.float32)
    y = xf * lax.rsqrt(jnp.mean(xf * xf, axis=-1, keepdims=True) + RMS_EPS)
    return (y * g.astype(jnp.float32)).astype(x.dtype)


def modulate(h, shift, scale):
    return h * (1 + scale) + shift


def dwconv3(u, w):
    up = jnp.pad(u, ((0, 0), (1, 1), (0, 0)))
    return up[:, :-2] * w[0] + up[:, 1:-1] * w[1] + up[:, 2:] * w[2]


def axial_rope_tables(S):
    t = jnp.arange(S)
    row = (t // GRID_W).astype(jnp.float32)
    col = (t % GRID_W).astype(jnp.float32)
    nf = HEAD_DIM // 4
    inv = ROPE_THETA ** (-jnp.arange(nf, dtype=jnp.float32) / nf)
    ang = jnp.stack([row[:, None] * inv, col[:, None] * inv], axis=1)
    return jnp.cos(ang), jnp.sin(ang)


def apply_axial_rope(x, cos, sin):
    B, S, H, hd = x.shape
    xa = x.reshape(B, S, H, 2, 2, hd // 4)
    x1, x2 = xa[..., 0, :], xa[..., 1, :]
    c = cos[None, :, None].astype(x.dtype)
    s = sin[None, :, None].astype(x.dtype)
    out = jnp.stack([x1 * c - x2 * s, x2 * c + x1 * s], axis=-2)
    return out.reshape(B, S, H, hd)


def ctx_attn(q, k, v, sink=None):
    B, L, H, hd = q.shape
    kvh = k.shape[2]
    g = H // kvh
    qg = q.reshape(B, L, kvh, g, hd)
    s = jnp.einsum('bqkgd,blkd->bkgql', qg, k).astype(jnp.float32) * (hd ** -0.5)
    if sink is not None:
        s_sink = jnp.broadcast_to(sink.astype(jnp.float32).reshape(1, kvh, g, 1, 1), s.shape[:-1] + (1,))
        s = jnp.concatenate([s, s_sink], axis=-1)
    p = jax.nn.softmax(s, axis=-1).astype(v.dtype)
    if sink is not None:
        p = p[..., :-1]
    o = jnp.einsum('bkgql,blkd->bqkgd', p, v)
    return o.reshape(B, L, H, hd)


def neighborhood_attn(q, k, v, kc, vc, rpb):
    B, S, H, hd = q.shape
    rows = S // GRID_W
    kh = min(NA_KH, rows)
    ncb = GRID_W // NA_QCB
    kcb = NA_QCB + NA_KW
    qg = q.reshape(B, rows, GRID_W, H, hd)
    kg = k.reshape(B, rows, GRID_W, H, hd)
    vg = v.reshape(B, rows, GRID_W, H, hd)
    qcol = jnp.arange(GRID_W).reshape(ncb, NA_QCB)
    cstart = jnp.clip(qcol - NA_KW // 2, 0, GRID_W - NA_KW)
    bstart = jnp.clip(jnp.arange(ncb) * NA_QCB - NA_KW // 2, 0, GRID_W - kcb)
    kcol = bstart[:, None] + jnp.arange(kcb)
    col_ok = (kcol[:, None, :] >= cstart[:, :, None]) & (kcol[:, None, :] < cstart[:, :, None] + NA_KW)
    col_rel = jnp.clip(kcol[:, None, :] - qcol[:, :, None], -(NA_KW - 1), NA_KW - 1) + NA_KW - 1
    scale = hd ** -0.5
    nloc = kh * kcb

    def row_block(r):
        rs = jnp.clip(r - kh // 2, 0, rows - kh)
        kr = lax.dynamic_slice_in_dim(kg, rs, kh, axis=1)[:, :, kcol]
        vr = lax.dynamic_slice_in_dim(vg, rs, kh, axis=1)[:, :, kcol]
        qr = lax.dynamic_index_in_dim(qg, r, axis=1, keepdims=False).reshape(B, ncb, NA_QCB, H, hd)
        row_rel = rs + jnp.arange(kh) - r + NA_KH - 1
        bias = rpb[:, row_rel[:, None, None, None], col_rel[None]]
        bias = bias.transpose(0, 2, 3, 1, 4).astype(jnp.float32)
        s_loc = jnp.einsum('bnqhd,bjnkhd->bhnqjk', qr, kr).astype(jnp.float32) * scale + bias
        s_loc = jnp.where(col_ok[:, :, None, :], s_loc, NEG_INF)
        s_ctx = jnp.einsum('bnqhd,blhd->bhnql', qr, kc).astype(jnp.float32) * scale
        s = jnp.concatenate([s_loc.reshape(B, H, ncb, NA_QCB, nloc), s_ctx], axis=-1)
        p = jax.nn.softmax(s, axis=-1).astype(v.dtype)
        p_loc = p[..., :nloc].reshape(B, H, ncb, NA_QCB, kh, kcb)
        o = (jnp.einsum('bhnqjk,bjnkhd->bnqhd', p_loc, vr)
             + jnp.einsum('bhnql,blhd->bnqhd', p[..., nloc:], vc))
        return o.reshape(B, GRID_W, H, hd)

    out = lax.map(row_block, jnp.arange(rows))
    return out.transpose(1, 0, 2, 3, 4).reshape(B, S, H, hd)


def window_gqa(q, k, v, kc, vc, sink):
    B, S, H, hd = q.shape
    kvh = k.shape[2]
    g = H // kvh
    nb = S // WA_BLOCK
    L = kc.shape[1]
    qb = q.reshape(B, nb, WA_BLOCK, kvh, g, hd)

    def band(t):
        tp = jnp.pad(t, ((0, 0), (WA_BLOCK, WA_BLOCK), (0, 0), (0, 0)))
        return jnp.concatenate(
            [tp[:, j * WA_BLOCK: j * WA_BLOCK + S].reshape(B, nb, WA_BLOCK, kvh, hd) for j in range(3)], axis=2)

    kband, vband = band(k), band(v)
    nl = 3 * WA_BLOCK
    ql = jnp.arange(WA_BLOCK)
    kl = jnp.arange(nl)
    off = kl[None, :] - WA_BLOCK - ql[:, None]
    kpos = jnp.arange(nb)[:, None] * WA_BLOCK - WA_BLOCK + kl[None, :]
    ok = (jnp.abs(off)[None] <= WA_WINDOW) & ((kpos >= 0) & (kpos < S))[:, None, :]
    scale = hd ** -0.5
    s_loc = jnp.einsum('bnqkgd,bnjkd->bkgnqj', qb, kband).astype(jnp.float32) * scale
    s_loc = jnp.where(ok, s_loc, NEG_INF)
    s_ctx = jnp.einsum('bnqkgd,blkd->bkgnql', qb, kc).astype(jnp.float32) * scale
    s_sink = jnp.broadcast_to(sink.astype(jnp.float32).reshape(1, kvh, g, 1, 1, 1), s_loc.shape[:-1] + (1,))
    p = jax.nn.softmax(jnp.concatenate([s_loc, s_ctx, s_sink], axis=-1), axis=-1).astype(v.dtype)
    o = (jnp.einsum('bkgnqj,bnjkd->bnqkgd', p[..., :nl], vband)
         + jnp.einsum('bkgnql,blkd->bnqkgd', p[..., nl:nl + L], vc))
    return o.reshape(B, S, H, hd)


def even_mixer(hl, hc, w_in, rpb, sink, w_out, cos, sin, ctx_live):
    B, S, _ = hl.shape

    def heads(p):
        T = p.shape[1]
        qa, ka, va, qb, kb, vb = jnp.split(p, EVEN_SPLITS, axis=-1)
        return (qa.reshape(B, T, NA_HEADS, HEAD_DIM), ka.reshape(B, T, NA_HEADS, HEAD_DIM),
                va.reshape(B, T, NA_HEADS, HEAD_DIM), qb.reshape(B, T, WA_HEADS, HEAD_DIM),
                kb.reshape(B, T, WA_KV_HEADS, HEAD_DIM), vb.reshape(B, T, WA_KV_HEADS, HEAD_DIM))

    qa_l, ka_l, va_l, qb_l, kb_l, vb_l = heads(hl @ w_in)
    qa_c, ka_c, va_c, qb_c, kb_c, vb_c = heads(hc @ w_in)
    qb_l = apply_axial_rope(qb_l, cos, sin)
    kb_l = apply_axial_rope(kb_l, cos, sin)
    oa = neighborhood_attn(qa_l, ka_l, va_l, ka_c, va_c, rpb)
    ob = window_gqa(qb_l, kb_l, vb_l, kb_c, vb_c, sink)
    yl = jnp.concatenate([oa.reshape(B, S, NA_WIDTH), ob.reshape(B, S, WA_Q_WIDTH)], axis=-1) @ w_out
    yc = None
    if ctx_live:
        L = hc.shape[1]
        oac = ctx_attn(qa_c, ka_c, va_c)
        obc = ctx_attn(qb_c, kb_c, vb_c, sink)
        yc = jnp.concatenate([oac.reshape(B, L, NA_WIDTH), obc.reshape(B, L, WA_Q_WIDTH)], axis=-1) @ w_out
    return yl, yc


def odd_mixer(h, w_in, conv_w, w_out):
    B, T, _ = h.shape
    u, bg, cg, hv = jnp.split(h @ w_in, ODD_SPLITS, axis=-1)
    uf = u.reshape(B, T, FNET_GROUPS, FNET_GROUP_DIM).astype(jnp.float32)
    f = jnp.fft.fft2(uf, axes=(1, 3), norm="ortho").real.astype(h.dtype).reshape(B, T, FNET_WIDTH)
    sc = bg * dwconv3(cg * hv, conv_w)
    return jnp.concatenate([f, sc], axis=-1) @ w_out


def conv_ffn(h, w_up, conv_w, w_down):
    u = dwconv3(h @ w_up, conv_w)
    gate, val = jnp.split(u, 2, axis=-1)
    return (jax.nn.silu(gate) * val) @ w_down


def setup_inputs(seed: int = 0) -> dict:
    key = jax.random.key(seed)
    ks = jax.random.split(key, 20)
    D = D_MODEL

    def nrm(k, shape, s):
        return jax.random.normal(k, shape, jnp.float32) * s

    return {
        "x": nrm(ks[0], (BATCH, SEQ, D), 1.0),
        "c": nrm(ks[1], (BATCH, D), 1.0),
        "ctx": nrm(ks[2], (BATCH, CTX_LEN, D), 1.0),
        "c_ctx": nrm(ks[3], (D,), 1.0),
        "w_mod": nrm(ks[4], (DEPTH, D, 6 * D), MOD_INIT * D ** -0.5),
        "b_mod": nrm(ks[5], (DEPTH, 6 * D), 0.02),
        "g_mix_pre": 1.0 + nrm(ks[6], (DEPTH, D), 0.05),
        "g_mix_post": 1.0 + nrm(ks[7], (DEPTH, D), 0.05),
        "g_ffn_pre": 1.0 + nrm(ks[8], (DEPTH, D), 0.05),
        "g_ffn_post": 1.0 + nrm(ks[9], (DEPTH, D), 0.05),
        "even_w_in": nrm(ks[10], (N_EVEN, D, EVEN_PROJ), D ** -0.5),
        "even_rpb": nrm(ks[11], (N_EVEN, NA_HEADS, 2 * NA_KH - 1, 2 * NA_KW - 1), 0.2),
        "even_sink": nrm(ks[12], (N_EVEN, WA_HEADS), 0.5),
        "even_w_out": nrm(ks[13], (N_EVEN, MIX_WIDTH, D), MIX_WIDTH ** -0.5),
        "odd_w_in": nrm(ks[14], (N_ODD, D, ODD_PROJ), D ** -0.5),
        "odd_conv": nrm(ks[15], (N_ODD, CONV_W, SC_WIDTH), CONV_W ** -0.5),
        "odd_w_out": nrm(ks[16], (N_ODD, MIX_WIDTH, D), MIX_WIDTH ** -0.5),
        "ffn_w_up": nrm(ks[17], (DEPTH, D, 2 * FFN_HIDDEN), D ** -0.5),
        "ffn_conv": nrm(ks[18], (DEPTH, CONV_W, 2 * FFN_HIDDEN), CONV_W ** -0.5),
        "ffn_w_down": nrm(ks[19], (DEPTH, FFN_HIDDEN, D), FFN_HIDDEN ** -0.5),
    }


def reference(x, c, ctx, c_ctx, w_mod, b_mod, g_mix_pre, g_mix_post, g_ffn_pre, g_ffn_post,
              even_w_in, even_rpb, even_sink, even_w_out, odd_w_in, odd_conv, odd_w_out,
              ffn_w_up, ffn_conv, ffn_w_down):
    S = x.shape[1]
    cos, sin = axial_rope_tables(S)
    xc = ctx
    silu_c = jax.nn.silu(c)
    silu_cc = jax.nn.silu(c_ctx)
    for i in range(DEPTH):
        ctx_live = any(j % 2 == 0 for j in range(i + 1, DEPTH))
        need_hc = (i % 2 == 0) or ctx_live
        sh_m, sc_m, gt_m, sh_f, sc_f, gt_f = jnp.split((silu_c @ w_mod[i] + b_mod[i])[:, None, :], 6, axis=-1)
        hl = modulate(rmsnorm(x, g_mix_pre[i]), sh_m, sc_m)
        hc = None
        if need_hc:
            csh_m, csc_m, cgt_m, csh_f, csc_f, cgt_f = jnp.split(silu_cc @ w_mod[i] + b_mod[i], 6, axis=-1)
            hc = modulate(rmsnorm(xc, g_mix_pre[i]), csh_m, csc_m)
        if i % 2 == 0:
            e = i // 2
            yl, yc = even_mixer(hl, hc, even_w_in[e], even_rpb[e], even_sink[e], even_w_out[e],
                                cos, sin, ctx_live)
        else:
            o = i // 2
            yl = odd_mixer(hl, odd_w_in[o], odd_conv[o], odd_w_out[o])
            yc = odd_mixer(hc, odd_w_in[o], odd_conv[o], odd_w_out[o]) if ctx_live else None
        x = x + gt_m * rmsnorm(yl, g_mix_post[i])
        hl = modulate(rmsnorm(x, g_ffn_pre[i]), sh_f, sc_f)
        x = x + gt_f * rmsnorm(conv_ffn(hl, ffn_w_up[i], ffn_conv[i], ffn_w_down[i]), g_ffn_post[i])
        if ctx_live:
            xc = xc + cgt_m * rmsnorm(yc, g_mix_post[i])
            hc = modulate(rmsnorm(xc, g_ffn_pre[i]), csh_f, csc_f)
            xc = xc + cgt_f * rmsnorm(conv_ffn(hc, ffn_w_up[i], ffn_conv[i], ffn_w_down[i]), g_ffn_post[i])
    return x
```

```python
import functools

import numpy as np
import jax
import jax.numpy as jnp
from jax import lax
from jax.experimental import pallas as pl
from jax.experimental.pallas import tpu as pltpu

F32 = jnp.float32
BF16 = jnp.bfloat16

GRID_W = 64
HEAD_DIM = 128
NA_KH = 8
NA_KW = 16
WA_KV_HEADS = 4
WA_WINDOW = 128
WA_BLOCK = 128
FNET_GROUP_DIM = 128
ROPE_THETA = 10000.0
RMS_EPS = 1e-6
NEG_INF = -1e30

V7X_VMEM_BYTES = 64 * 1024 * 1024
VMEM_LIMIT = V7X_VMEM_BYTES - 8 * 1024 * 1024
LANES = 128

NA_ROW_BLOCK = 4


def _params(n_grid):
    return pltpu.CompilerParams(dimension_semantics=("arbitrary",) * n_grid,
                                vmem_limit_bytes=VMEM_LIMIT)


def _pick(n, pref):
    if n <= pref:
        return n
    t = pref
    while n % t:
        t //= 2
    return t


def _mod_kernel(c_ref, w_ref, b_ref, o_ref):
    c = c_ref[...]
    s = (c / (1.0 + jnp.exp(-c))).astype(BF16)
    y = jnp.dot(s, w_ref[...].astype(BF16), preferred_element_type=F32)
    o_ref[...] = y + b_ref[...]


def _modulation(cond, w_mod, b_mod):
    depth, d, n = w_mod.shape
    rows = cond.shape[0]
    tn = _pick(n, 512)
    return pl.pallas_call(
        _mod_kernel,
        grid=(depth, n // tn),
        in_specs=[pl.BlockSpec((rows, d), lambda l, j: (0, 0)),
                  pl.BlockSpec((None, d, tn), lambda l, j: (l, 0, j)),
                  pl.BlockSpec((None, 1, tn), lambda l, j: (l, 0, j))],
        out_specs=pl.BlockSpec((None, rows, tn), lambda l, j: (l, 0, j)),
        out_shape=jax.ShapeDtypeStruct((depth, rows, n), F32),
        compiler_params=_params(2),
    )(cond, w_mod, b_mod.reshape(depth, 1, n))


def _rms(v, g):
    return v * lax.rsqrt(jnp.mean(v * v, axis=-1, keepdims=True) + RMS_EPS) * g


def _mod_row(ref, fixed_row):
    row = pl.program_id(0) if fixed_row is None else fixed_row
    return ref[pl.ds(row, 1), :]


def _prenorm_kernel(x_ref, g_ref, sh_ref, sc_ref, h_ref, *, fixed_row):
    sh = _mod_row(sh_ref, fixed_row)
    sc = _mod_row(sc_ref, fixed_row)
    h_ref[...] = (_rms(x_ref[...], g_ref[...]) * (1.0 + sc) + sh).astype(h_ref.dtype)


def _prenorm(x, g, mod, layer, slot, fixed_row=None):
    b, t, d = x.shape
    rows = mod.shape[1]
    tr = _pick(t, 256)
    return pl.pallas_call(
        functools.partial(_prenorm_kernel, fixed_row=fixed_row),
        grid=(b, t // tr),
        in_specs=[pl.BlockSpec((None, tr, d), lambda i, j: (i, j, 0)),
                  pl.BlockSpec((1, d), lambda i, j: (0, 0)),
                  pl.BlockSpec((None, rows, d), lambda i, j: (layer, 0, slot)),
                  pl.BlockSpec((None, rows, d), lambda i, j: (layer, 0, slot + 1))],
        out_specs=pl.BlockSpec((None, tr, d), lambda i, j: (i, j, 0)),
        out_shape=jax.ShapeDtypeStruct((b, t, d), BF16),
        compiler_params=_params(2),
    )(x, g.reshape(1, d), mod, mod)


def _resid_kernel(x_ref, y_ref, gp_ref, gt_ref, *rest, fixed_row, with_next):
    gt = _mod_row(gt_ref, fixed_row)
    xn = x_ref[...] + gt * _rms(y_ref[...], gp_ref[...])
    if with_next:
        gn_ref, sh_ref, sc_ref, xo_ref, h_ref = rest
        sh = _mod_row(sh_ref, fixed_row)
        sc = _mod_row(sc_ref, fixed_row)
        h_ref[...] = (_rms(xn, gn_ref[...]) * (1.0 + sc) + sh).astype(h_ref.dtype)
    else:
        (xo_ref,) = rest
    xo_ref[...] = xn


def _residual(x, y, g_post, mod, layer, gate_slot, nxt=None, fixed_row=None):
    b, t, d = x.shape
    rows = mod.shape[1]
    tr = _pick(t, 256)
    row_spec = pl.BlockSpec((None, tr, d), lambda i, j: (i, j, 0))
    vec_spec = pl.BlockSpec((1, d), lambda i, j: (0, 0))

    def mod_spec(l, s):
        return pl.BlockSpec((None, rows, d), lambda i, j: (l, 0, s))

    in_specs = [row_spec, row_spec, vec_spec, mod_spec(layer, gate_slot)]
    args = [x, y, g_post.reshape(1, d), mod]
    out_specs = [row_spec]
    out_shape = [jax.ShapeDtypeStruct((b, t, d), F32)]
    if nxt is not None:
        g_pre, nl, ns = nxt
        in_specs += [vec_spec, mod_spec(nl, ns), mod_spec(nl, ns + 1)]
        args += [g_pre.reshape(1, d), mod, mod]
        out_specs.append(row_spec)
        out_shape.append(jax.ShapeDtypeStruct((b, t, d), BF16))
    out = pl.pallas_call(
        functools.partial(_resid_kernel, fixed_row=fixed_row, with_next=nxt is not None),
        grid=(b, t // tr),
        in_specs=in_specs, out_specs=out_specs, out_shape=out_shape,
        input_output_aliases={0: 0},
        compiler_params=_params(2),
    )(*args)
    return (out[0], out[1]) if nxt is not None else (out[0], None)


def _mm_kernel(a_ref, w_ref, o_ref, *scratch):
    if scratch:
        (wbf,) = scratch

        @pl.when(pl.program_id(2) == 0)
        def _():
            wbf[...] = w_ref[...].astype(BF16)

        w = wbf[...]
    else:
        w = w_ref[...]
    o_ref[...] = jnp.dot(a_ref[...], w, preferred_element_type=F32).astype(o_ref.dtype)


def _matmul(a, w, out_dtype, *, w_index=None, out=None, out_col=0, tm=1024, tn=512):
    ga, m, k = a.shape
    gw, _, n = w.shape
    g = ga if w_index is not None else gw
    tm = _pick(m, tm)
    tn = _pick(n, tn)
    a_map = (lambda b, j, i: (b, i, 0)) if ga == g and ga > 1 else (lambda b, j, i: (0, i, 0))
    if w_index is not None:
        w_map = lambda b, j, i: (w_index, 0, j)
    else:
        w_map = lambda b, j, i: (b, 0, j)
    col0 = out_col // tn
    assert out_col % tn == 0
    o_map = lambda b, j, i: (b, i, col0 + j)
    scratch = [pltpu.VMEM((k, tn), BF16)] if w.dtype != BF16 else []
    in_specs = [pl.BlockSpec((None, tm, k), a_map), pl.BlockSpec((None, k, tn), w_map)]
    args = [a, w]
    aliases = {}
    if out is None:
        out_shape = jax.ShapeDtypeStruct((g, m, n), out_dtype)
    else:
        out_shape = jax.ShapeDtypeStruct(out.shape, out.dtype)
        in_specs.append(pl.BlockSpec(memory_space=pl.ANY))
        args.append(out)
        aliases = {2: 0}
    kern = _mm_kernel if out is None else (lambda a_r, w_r, _o_in, o_r, *s: _mm_kernel(a_r, w_r, o_r, *s))
    return pl.pallas_call(
        kern,
        grid=(g, n // tn, m // tm),
        in_specs=in_specs,
        out_specs=pl.BlockSpec((None, tm, tn), o_map),
        out_shape=out_shape,
        scratch_shapes=scratch,
        input_output_aliases=aliases,
        compiler_params=_params(3),
    )(*args)


def _na_geometry(rows, rb):
    kh = min(NA_KH, rows)
    kr = min(rb + kh, rows)
    nblk = rows // rb
    qcol = np.arange(GRID_W)
    cstart = np.clip(qcol - NA_KW // 2, 0, GRID_W - NA_KW)
    kcol = np.arange(GRID_W)
    col_ok = (kcol[None, :] >= cstart[:, None]) & (kcol[None, :] < cstart[:, None] + NA_KW)
    col_rel = np.clip(kcol[None, :] - qcol[:, None], -(NA_KW - 1), NA_KW - 1) + NA_KW - 1
    slab = np.clip(np.arange(nblk) * rb - kh // 2, 0, rows - kr)
    tables = []
    for j in range(nblk):
        r = j * rb + np.arange(rb)
        rs = np.clip(r - kh // 2, 0, rows - kh)
        key_row = slab[j] + np.arange(kr)
        row_ok = (key_row[None, :] >= rs[:, None]) & (key_row[None, :] < rs[:, None] + kh)
        row_rel = np.clip(key_row[None, :] - r[:, None] + NA_KH - 1, 0, 2 * NA_KH - 2)
        tables.append((row_ok, row_rel))
    variants, variant_of = [], []
    for tab in tables:
        for vi, v in enumerate(variants):
            if np.array_equal(v[0], tab[0]) and np.array_equal(np.where(v[0], v[1], 0), np.where(tab[0], tab[1], 0)):
                variant_of.append(vi)
                break
        else:
            variants.append(tab)
            variant_of.append(len(variants) - 1)
    return kr, slab, np.asarray(variant_of), variants, col_ok, col_rel


def _na_bias(rpb, rows, rb):
    kr, slab, variant_of, variants, col_ok, col_rel = _na_geometry(rows, rb)
    h = rpb.shape[0]
    e = jnp.take(rpb, jnp.asarray(col_rel.reshape(-1)), axis=2)
    e = e.reshape(h, 2 * NA_KH - 1, GRID_W, GRID_W)
    out = []
    for row_ok, row_rel in variants:
        t = jnp.take(e, jnp.asarray(row_rel.reshape(-1)), axis=1)
        t = t.reshape(h, rb, kr, GRID_W, GRID_W)
        ok = row_ok[:, :, None, None] & col_ok[None, None]
        t = jnp.where(jnp.asarray(ok)[None], t, NEG_INF)
        out.append(t.transpose(0, 1, 3, 2, 4).reshape(h, rb * GRID_W, kr * GRID_W))
    return jnp.stack(out, axis=1).astype(F32), kr, slab, variant_of


def _nt_dot(a, b):
    return lax.dot_general(a, b, (((1,), (1,)), ((), ())), preferred_element_type=F32)


def _na_kernel(slab_ref, var_ref, q_ref, k_ref, v_ref, kc_ref, vc_ref, bias_ref, _mix_in, o_ref, *, kr):
    del var_ref
    j = pl.program_id(2)
    ks = pl.multiple_of(slab_ref[j] * GRID_W, GRID_W)
    nk = kr * GRID_W
    scale = HEAD_DIM ** -0.5
    q = q_ref[...]
    s_loc = _nt_dot(q, k_ref[pl.ds(ks, nk), :]) * scale + bias_ref[...]
    s_ctx = _nt_dot(q, kc_ref[...]) * scale
    m = jnp.maximum(jnp.max(s_loc, axis=-1, keepdims=True), jnp.max(s_ctx, axis=-1, keepdims=True))
    p_loc = jnp.exp(s_loc - m)
    p_ctx = jnp.exp(s_ctx - m)
    den = jnp.sum(p_loc, axis=-1, keepdims=True) + jnp.sum(p_ctx, axis=-1, keepdims=True)
    o = (jnp.dot(p_loc.astype(BF16), v_ref[pl.ds(ks, nk), :], preferred_element_type=F32)
         + jnp.dot(p_ctx.astype(BF16), vc_ref[...], preferred_element_type=F32))
    o_ref[...] = (o / den).astype(o_ref.dtype)


def _neighborhood_attn(p, pc, rpb, mix, n_heads):
    b, t, _ = p.shape
    l = pc.shape[1]
    rows = t // GRID_W
    rb = min(NA_ROW_BLOCK, rows)
    bias, kr, slab, variant_of = _na_bias(rpb, rows, rb)
    nblk = rows // rb
    tq = rb * GRID_W
    hd = HEAD_DIM
    grid_spec = pltpu.PrefetchScalarGridSpec(
        num_scalar_prefetch=2,
        grid=(b, n_heads, nblk),
        in_specs=[
            pl.BlockSpec((None, tq, hd), lambda i, h, j, s, v: (i, j, h)),
            pl.BlockSpec((None, t, hd), lambda i, h, j, s, v: (i, 0, n_heads + h)),
            pl.BlockSpec((None, t, hd), lambda i, h, j, s, v: (i, 0, 2 * n_heads + h)),
            pl.BlockSpec((None, l, hd), lambda i, h, j, s, v: (i, 0, n_heads + h)),
            pl.BlockSpec((None, l, hd), lambda i, h, j, s, v: (i, 0, 2 * n_heads + h)),
            pl.BlockSpec((None, None, tq, kr * GRID_W), lambda i, h, j, s, v: (h, v[j], 0, 0)),
            pl.BlockSpec(memory_space=pl.ANY),
        ],
        out_specs=pl.BlockSpec((None, tq, hd), lambda i, h, j, s, v: (i, j, h)),
    )
    return pl.pallas_call(
        functools.partial(_na_kernel, kr=kr),
        grid_spec=grid_spec,
        out_shape=jax.ShapeDtypeStruct(mix.shape, mix.dtype),
        input_output_aliases={8: 0},
        compiler_params=_params(3),
    )(jnp.asarray(slab, jnp.int32), jnp.asarray(variant_of, jnp.int32), p, p, p, pc, pc, bias, mix)


def _rope_tables(s):
    tpos = jnp.arange(s)
    row = (tpos // GRID_W).astype(F32)
    col = (tpos % GRID_W).astype(F32)
    nf = HEAD_DIM // 4
    inv = ROPE_THETA ** (-jnp.arange(nf, dtype=F32) / nf)
    lane = np.arange(HEAD_DIM)
    axis = lane // (2 * nf)
    second = (lane % (2 * nf)) >= nf
    freq = lane % nf
    pos = jnp.where(jnp.asarray(axis)[None, :] == 0, row[:, None], col[:, None])
    ang = pos * inv[jnp.asarray(freq)][None, :]
    cos, sin = jnp.cos(ang), jnp.sin(ang)
    sec = jnp.asarray(second)[None, :]
    return cos, jnp.where(sec, 0.0, -sin), jnp.where(sec, sin, 0.0)


def _rope(x, c, s1, s2):
    nf = HEAD_DIM // 4
    return x * c + pltpu.roll(x, HEAD_DIM - nf, axis=1) * s1 + pltpu.roll(x, nf, axis=1) * s2


def _wa_kernel(sink_ref, q_ref, k_ref, v_ref, kc_ref, vc_ref, c_ref, s1_ref, s2_ref, _mix_in, o_ref,
               *, group, seq):
    kvh = pl.program_id(1)
    n = pl.program_id(2)
    wb, hd = WA_BLOCK, HEAD_DIM
    nl = 3 * wb
    q0 = pl.multiple_of(n * wb, wb)
    bs = pl.multiple_of(jnp.clip(n * wb - wb, 0, seq - nl), wb)
    scale = hd ** -0.5
    cq, s1q, s2q = c_ref[pl.ds(q0, wb), :], s1_ref[pl.ds(q0, wb), :], s2_ref[pl.ds(q0, wb), :]
    qblk = q_ref[...]
    q = jnp.concatenate(
        [_rope(qblk[:, g * hd:(g + 1) * hd].astype(F32), cq, s1q, s2q).astype(BF16) for g in range(group)],
        axis=0)
    kb = _rope(k_ref[pl.ds(bs, nl), :].astype(F32), c_ref[pl.ds(bs, nl), :], s1_ref[pl.ds(bs, nl), :],
               s2_ref[pl.ds(bs, nl), :]).astype(BF16)
    s_loc = _nt_dot(q, kb) * scale
    rowi = lax.broadcasted_iota(jnp.int32, (group * wb, nl), 0)
    kpos = bs + lax.broadcasted_iota(jnp.int32, (group * wb, nl), 1)
    qpos = q0 + (rowi & (wb - 1))
    s_loc = jnp.where(jnp.abs(kpos - qpos) <= WA_WINDOW, s_loc, NEG_INF)
    s_ctx = _nt_dot(q, kc_ref[...]) * scale
    gi = lax.broadcasted_iota(jnp.int32, (group * wb, 1), 0) // wb
    sink = jnp.zeros((group * wb, 1), F32)
    for g in range(group):
        sink = jnp.where(gi == g, sink_ref[kvh * group + g], sink)
    m = jnp.maximum(jnp.maximum(jnp.max(s_loc, axis=-1, keepdims=True),
                                jnp.max(s_ctx, axis=-1, keepdims=True)), sink)
    p_loc = jnp.exp(s_loc - m)
    p_ctx = jnp.exp(s_ctx - m)
    den = (jnp.sum(p_loc, axis=-1, keepdims=True) + jnp.sum(p_ctx, axis=-1, keepdims=True)
           + jnp.exp(sink - m))
    o = (jnp.dot(p_loc.astype(BF16), v_ref[pl.ds(bs, nl), :], preferred_element_type=F32)
         + jnp.dot(p_ctx.astype(BF16), vc_ref[...], preferred_element_type=F32))
    o = (o / den).astype(o_ref.dtype)
    for g in range(group):
        o_ref[:, g * hd:(g + 1) * hd] = o[g * wb:(g + 1) * wb, :]


def _window_gqa(p, pc, sink, mix, q_col, k_col, v_col, out_col, n_heads, rope):
    b, t, _ = p.shape
    l = pc.shape[1]
    kvh = WA_KV_HEADS
    group = n_heads // kvh
    hd, wb = HEAD_DIM, WA_BLOCK
    gw = group * hd
    assert q_col % gw == 0 and out_col % gw == 0 and t % wb == 0 and t >= 3 * wb
    cos, s1, s2 = rope
    tab_spec = pl.BlockSpec((t, hd), lambda i, k, n: (0, 0))
    return pl.pallas_call(
        functools.partial(_wa_kernel, group=group, seq=t),
        grid=(b, kvh, t // wb),
        in_specs=[
            pl.BlockSpec(memory_space=pltpu.SMEM),
            pl.BlockSpec((None, wb, gw), lambda i, k, n: (i, n, q_col // gw + k)),
            pl.BlockSpec((None, t, hd), lambda i, k, n: (i, 0, k_col // hd + k)),
            pl.BlockSpec((None, t, hd), lambda i, k, n: (i, 0, v_col // hd + k)),
            pl.BlockSpec((None, l, hd), lambda i, k, n: (i, 0, k_col // hd + k)),
            pl.BlockSpec((None, l, hd), lambda i, k, n: (i, 0, v_col // hd + k)),
            tab_spec, tab_spec, tab_spec,
            pl.BlockSpec(memory_space=pl.ANY),
        ],
        out_specs=pl.BlockSpec((None, wb, gw), lambda i, k, n: (i, n, out_col // gw + k)),
        out_shape=jax.ShapeDtypeStruct(mix.shape, mix.dtype),
        input_output_aliases={9: 0},
        compiler_params=_params(3),
    )(sink, p, p, p, pc, pc, cos, s1, s2, mix)


def _ctx_kernel(sink_ref, q_ref, k_ref, v_ref, o_ref):
    h = pl.program_id(1)
    scale = HEAD_DIM ** -0.5
    s = _nt_dot(q_ref[...], k_ref[...]) * scale
    sink = sink_ref[h]
    m = jnp.maximum(jnp.max(s, axis=-1, keepdims=True), sink)
    p = jnp.exp(s - m)
    den = jnp.sum(p, axis=-1, keepdims=True) + jnp.exp(sink - m)
    o = jnp.dot(p.astype(BF16), v_ref[...], preferred_element_type=F32)
    o_ref[...] = (o / den).astype(o_ref.dtype)


def _ctx_attn(pc, sink, na_heads, wa_heads):
    b, l, _ = pc.shape
    hd = HEAD_DIM
    group = wa_heads // WA_KV_HEADS
    qb0 = 3 * na_heads
    kb0 = qb0 + wa_heads
    vb0 = kb0 + WA_KV_HEADS
    sink_all = jnp.concatenate([jnp.full((na_heads,), NEG_INF, F32), sink.astype(F32)])

    def qmap(i, h):
        return (i, 0, jnp.where(h < na_heads, h, qb0 + h - na_heads))

    def kmap(i, h):
        return (i, 0, jnp.where(h < na_heads, na_heads + h, kb0 + (h - na_heads) // group))

    def vmap(i, h):
        return (i, 0, jnp.where(h < na_heads, 2 * na_heads + h, vb0 + (h - na_heads) // group))

    return pl.pallas_call(
        _ctx_kernel,
        grid=(b, na_heads + wa_heads),
        in_specs=[pl.BlockSpec(memory_space=pltpu.SMEM),
                  pl.BlockSpec((None, l, hd), qmap),
                  pl.BlockSpec((None, l, hd), kmap),
                  pl.BlockSpec((None, l, hd), vmap)],
        out_specs=pl.BlockSpec((None, l, hd), lambda i, h: (i, 0, h)),
        out_shape=jax.ShapeDtypeStruct((b, l, (na_heads + wa_heads) * hd), BF16),
        compiler_params=_params(2),
    )(sink_all, pc, pc, pc)


def _dft_cos_sin(n):
    lo = 1
    while lo * lo < n:
        lo *= 2
    hi = n // lo
    k = jnp.arange(n, dtype=jnp.int32)
    a = (2.0 * np.pi / n) * ((k[:, None] * (jnp.arange(hi, dtype=jnp.int32) * lo)[None, :]) % n).astype(F32)
    bb = (2.0 * np.pi / n) * ((k[:, None] * jnp.arange(lo, dtype=jnp.int32)[None, :]) % n).astype(F32)
    ca, sa, cb, sb = jnp.cos(a)[:, :, None], jnp.sin(a)[:, :, None], jnp.cos(bb)[:, None, :], jnp.sin(bb)[:, None, :]
    return (ca * cb - sa * sb).reshape(n, n), (sa * cb + ca * sb).reshape(n, n)


def _group_dft_kernel(u_ref, cs_ref, o_ref, *, groups):
    gd = FNET_GROUP_DIM
    cs = cs_ref[...]
    for g in range(groups):
        r = jnp.dot(u_ref[:, g * gd:(g + 1) * gd], cs, preferred_element_type=F32).astype(o_ref.dtype)
        o_ref[0, :, g * gd:(g + 1) * gd] = r[:, :gd]
        o_ref[1, :, g * gd:(g + 1) * gd] = r[:, gd:]


def _group_dft(p, width):
    b, t, _ = p.shape
    gd = FNET_GROUP_DIM
    c, s = _dft_cos_sin(gd)
    cs = (jnp.concatenate([c, s], axis=1) * (gd ** -0.5)).astype(BF16)
    tt = _pick(t, 512)
    return pl.pallas_call(
        functools.partial(_group_dft_kernel, groups=width // gd),
        grid=(b, t // tt),
        in_specs=[pl.BlockSpec((None, tt, width), lambda i, j: (i, j, 0)),
                  pl.BlockSpec((gd, 2 * gd), lambda i, j: (0, 0))],
        out_specs=pl.BlockSpec((None, 2, tt, width), lambda i, j: (i, 0, j, 0)),
        out_shape=jax.ShapeDtypeStruct((b, 2, t, width), BF16),
        compiler_params=_params(2),
    )(p, cs)


def _seq_dft_matrix(t):
    c, s = _dft_cos_sin(t)
    return (jnp.concatenate([c, -s], axis=1) * (t ** -0.5)).astype(BF16)[None]


def _dwconv3(z, w_ref):
    t = z.shape[0]
    pos = lax.broadcasted_iota(jnp.int32, z.shape, 0)
    zp = jnp.where(pos == 0, 0.0, pltpu.roll(z, 1, axis=0))
    zn = jnp.where(pos == t - 1, 0.0, pltpu.roll(z, t - 1, axis=0))
    return zp * w_ref[0:1, :] + z * w_ref[1:2, :] + zn * w_ref[2:3, :]


def _shortconv_kernel(bg_ref, cg_ref, hv_ref, w_ref, _mix_in, o_ref):
    z = cg_ref[...].astype(F32) * hv_ref[...].astype(F32)
    o_ref[...] = (bg_ref[...].astype(F32) * _dwconv3(z, w_ref)).astype(o_ref.dtype)


def _shortconv(p, conv_w, layer, mix, col0, width):
    b, t, _ = p.shape
    tc = LANES
    nb = width // tc
    c0 = col0 // tc
    return pl.pallas_call(
        _shortconv_kernel,
        grid=(b, nb),
        in_specs=[pl.BlockSpec((None, t, tc), lambda i, j: (i, 0, c0 + j)),
                  pl.BlockSpec((None, t, tc), lambda i, j: (i, 0, c0 + nb + j)),
                  pl.BlockSpec((None, t, tc), lambda i, j: (i, 0, c0 + 2 * nb + j)),
                  pl.BlockSpec((None, 3, tc), lambda i, j: (layer, 0, j)),
                  pl.BlockSpec(memory_space=pl.ANY)],
        out_specs=pl.BlockSpec((None, t, tc), lambda i, j: (i, 0, c0 + j)),
        out_shape=jax.ShapeDtypeStruct(mix.shape, mix.dtype),
        input_output_aliases={4: 0},
        compiler_params=_params(2),
    )(p, p, p, conv_w, mix)


def _ffn_gate_kernel(g_ref, v_ref, wg_ref, wv_ref, o_ref):
    gate = _dwconv3(g_ref[...].astype(F32), wg_ref)
    val = _dwconv3(v_ref[...].astype(F32), wv_ref)
    o_ref[...] = (gate / (1.0 + jnp.exp(-gate)) * val).astype(o_ref.dtype)


def _ffn_gate(u, conv_w, layer):
    b, t, f2 = u.shape
    f = f2 // 2
    tc = LANES
    nb = f // tc
    return pl.pallas_call(
        _ffn_gate_kernel,
        grid=(b, nb),
        in_specs=[pl.BlockSpec((None, t, tc), lambda i, j: (i, 0, j)),
                  pl.BlockSpec((None, t, tc), lambda i, j: (i, 0, nb + j)),
                  pl.BlockSpec((None, 3, tc), lambda i, j: (layer, 0, j)),
                  pl.BlockSpec((None, 3, tc), lambda i, j: (layer, 0, nb + j))],
        out_specs=pl.BlockSpec((None, t, tc), lambda i, j: (i, 0, j)),
        out_shape=jax.ShapeDtypeStruct((b, t, f), BF16),
        compiler_params=_params(2),
    )(u, u, conv_w, conv_w)


def _proj(h, w, layer, out_dtype, **kw):
    b, t, k = h.shape
    y = _matmul(h.reshape(1, b * t, k), w, out_dtype, w_index=layer, **kw)
    return y.reshape(b, t, -1)


def _even_mixer(hl, hc, w_in, rpb, sink, w_out, e, rope, ctx_live):
    b, t, d = hl.shape
    na_heads = rpb.shape[1]
    wa_heads = sink.shape[1]
    na_w = na_heads * HEAD_DIM
    wa_w = wa_heads * HEAD_DIM
    kv_w = WA_KV_HEADS * HEAD_DIM
    p = _proj(hl, w_in, e, BF16)
    pc = _proj(hc, w_in, e, BF16)
    mix = jnp.zeros((b, t, na_w + wa_w), BF16)
    mix = _neighborhood_attn(p, pc, rpb[e], mix, na_heads)
    mix = _window_gqa(p, pc, sink[e].astype(F32), mix, 3 * na_w, 3 * na_w + wa_w, 3 * na_w + wa_w + kv_w,
                      na_w, wa_heads, rope)
    yl = _proj(mix, w_out, e, F32)
    yc = None
    if ctx_live:
        yc = _proj(_ctx_attn(pc, sink[e], na_heads, wa_heads), w_out, e, F32)
    return yl, yc


def _odd_mixer(h, w_in, conv_w, w_out, o):
    b, t, d = h.shape
    sc_w = conv_w.shape[2]
    f_w = w_in.shape[2] - 3 * sc_w
    p = _proj(h, w_in, o, BF16)
    v = _group_dft(p, f_w).reshape(b, 2 * t, f_w)
    mix = jnp.zeros((b, t, f_w + sc_w), BF16)
    mix = _matmul(_seq_dft_matrix(t), v, BF16, out=mix, out_col=0, tm=512, tn=512)
    mix = _shortconv(p, conv_w, o, mix, f_w, sc_w)
    return _proj(mix, w_out, o, F32)


def _conv_ffn(h, w_up, conv_w, w_down, i):
    u = _proj(h, w_up, i, BF16)
    a = _ffn_gate(u, conv_w, i)
    return _proj(a, w_down, i, F32, tm=512)


def kernel(x, c, ctx, c_ctx, w_mod, b_mod, g_mix_pre, g_mix_post, g_ffn_pre, g_ffn_post, even_w_in, even_rpb, even_sink, even_w_out, odd_w_in, odd_conv, odd_w_out, ffn_w_up, ffn_conv, ffn_w_down):
    b, t, d = x.shape
    depth = w_mod.shape[0]
    rows = -(-(b + 1) // 8) * 8
    cond = jnp.zeros((rows, d), F32).at[:b].set(c).at[b].set(c_ctx)
    mod = _modulation(cond, w_mod, b_mod)
    rope = _rope_tables(t)
    xc = ctx
    hl = _prenorm(x, g_mix_pre[0], mod, 0, 0)
    for i in range(depth):
        ctx_live = any(j % 2 == 0 for j in range(i + 1, depth))
        need_hc = (i % 2 == 0) or ctx_live
        hc = _prenorm(xc, g_mix_pre[i], mod, i, 0, fixed_row=b) if need_hc else None
        if i % 2 == 0:
            e = i // 2
            yl, yc = _even_mixer(hl, hc, even_w_in, even_rpb, even_sink, even_w_out, e, rope, ctx_live)
        else:
            o = i // 2
            yl = _odd_mixer(hl, odd_w_in, odd_conv, odd_w_out, o)
            yc = _odd_mixer(hc, odd_w_in, odd_conv, odd_w_out, o) if ctx_live else None
        x, hl = _residual(x, yl, g_mix_post[i], mod, i, 2, nxt=(g_ffn_pre[i], i, 3))
        yf = _conv_ffn(hl, ffn_w_up, ffn_conv, ffn_w_down, i)
        nxt = (g_mix_pre[i + 1], i + 1, 0) if i + 1 < depth else None
        x, hl = _residual(x, yf, g_ffn_post[i], mod, i, 5, nxt=nxt)
        if ctx_live:
            xc, hcf = _residual(xc, yc, g_mix_post[i], mod, i, 2, nxt=(g_ffn_pre[i], i, 3), fixed_row=b)
            yfc = _conv_ffn(hcf, ffn_w_up, ffn_conv, ffn_w_down, i)
            xc, _ = _residual(xc, yfc, g_ffn_post[i], mod, i, 5, fixed_row=b)
    return x
```

```python
import functools

import numpy as np
import jax
import jax.numpy as jnp
from jax import lax
from jax.experimental import pallas as pl
from jax.experimental.pallas import tpu as pltpu

F32 = jnp.float32
BF16 = jnp.bfloat16

GRID_W = 64
HEAD_DIM = 128
NA_KH = 8
NA_KW = 16
WA_KV_HEADS = 4
WA_WINDOW = 128
WA_BLOCK = 128
FNET_GROUP_DIM = 128
ROPE_THETA = 10000.0
RMS_EPS = 1e-6
NEG_INF = -1e30

V7X_VMEM_BYTES = 64 * 1024 * 1024
VMEM_LIMIT = V7X_VMEM_BYTES - 8 * 1024 * 1024
LANES = 128

NA_ROW_BLOCK = 4
NA_HEADS_PER_STEP = 4
WA_KV_PER_STEP = 2
F32_SUBLANES = 8


def _params(n_grid):
    return pltpu.CompilerParams(dimension_semantics=("arbitrary",) * n_grid,
                                vmem_limit_bytes=VMEM_LIMIT)


def _pick(n, pref):
    if n <= pref:
        return n
    t = pref
    while n % t:
        t //= 2
    return t


def _mod_kernel(c_ref, w_ref, b_ref, o_ref):
    c = c_ref[...]
    s = (c / (1.0 + jnp.exp(-c))).astype(BF16)
    y = jnp.dot(s, w_ref[...].astype(BF16), preferred_element_type=F32)
    o_ref[...] = y + b_ref[...]


def _modulation(cond, w_mod, b_mod):
    depth, d, n = w_mod.shape
    rows = cond.shape[0]
    tn = _pick(n, 512)
    return pl.pallas_call(
        _mod_kernel,
        grid=(depth, n // tn),
        in_specs=[pl.BlockSpec((rows, d), lambda l, j: (0, 0)),
                  pl.BlockSpec((None, d, tn), lambda l, j: (l, 0, j)),
                  pl.BlockSpec((None, 1, tn), lambda l, j: (l, 0, j))],
        out_specs=pl.BlockSpec((None, rows, tn), lambda l, j: (l, 0, j)),
        out_shape=jax.ShapeDtypeStruct((depth, rows, n), F32),
        compiler_params=_params(2),
        name="adaln_mod",
    )(cond, w_mod, b_mod.reshape(depth, 1, n))


def _rms(v, g):
    return v * lax.rsqrt(jnp.mean(v * v, axis=-1, keepdims=True) + RMS_EPS) * g


def _mod_row(ref, fixed_row):
    row = pl.program_id(0) if fixed_row is None else fixed_row
    return ref[pl.ds(row, 1), :]


def _prenorm_kernel(x_ref, g_ref, sh_ref, sc_ref, h_ref, *, fixed_row):
    sh = _mod_row(sh_ref, fixed_row)
    sc = _mod_row(sc_ref, fixed_row)
    h_ref[...] = (_rms(x_ref[...], g_ref[...]) * (1.0 + sc) + sh).astype(h_ref.dtype)


def _prenorm(x, g, mod, layer, slot, fixed_row=None):
    b, t, d = x.shape
    rows = mod.shape[1]
    tr = _pick(t, 256)
    return pl.pallas_call(
        functools.partial(_prenorm_kernel, fixed_row=fixed_row),
        grid=(b, t // tr),
        in_specs=[pl.BlockSpec((None, tr, d), lambda i, j: (i, j, 0)),
                  pl.BlockSpec((1, d), lambda i, j: (0, 0)),
                  pl.BlockSpec((None, rows, d), lambda i, j: (layer, 0, slot)),
                  pl.BlockSpec((None, rows, d), lambda i, j: (layer, 0, slot + 1))],
        out_specs=pl.BlockSpec((None, tr, d), lambda i, j: (i, j, 0)),
        out_shape=jax.ShapeDtypeStruct((b, t, d), BF16),
        compiler_params=_params(2),
        name="prenorm",
    )(x, g.reshape(1, d), mod, mod)


def _resid_kernel(x_ref, y_ref, gp_ref, gt_ref, *rest, fixed_row, with_next):
    gt = _mod_row(gt_ref, fixed_row)
    xn = x_ref[...] + gt * _rms(y_ref[...], gp_ref[...])
    if with_next:
        gn_ref, sh_ref, sc_ref, xo_ref, h_ref = rest
        sh = _mod_row(sh_ref, fixed_row)
        sc = _mod_row(sc_ref, fixed_row)
        h_ref[...] = (_rms(xn, gn_ref[...]) * (1.0 + sc) + sh).astype(h_ref.dtype)
    else:
        (xo_ref,) = rest
    xo_ref[...] = xn


def _residual(x, y, g_post, mod, layer, gate_slot, nxt=None, fixed_row=None):
    b, t, d = x.shape
    rows = mod.shape[1]
    tr = _pick(t, 256)
    row_spec = pl.BlockSpec((None, tr, d), lambda i, j: (i, j, 0))
    vec_spec = pl.BlockSpec((1, d), lambda i, j: (0, 0))

    def mod_spec(l, s):
        return pl.BlockSpec((None, rows, d), lambda i, j: (l, 0, s))

    in_specs = [row_spec, row_spec, vec_spec, mod_spec(layer, gate_slot)]
    args = [x, y, g_post.reshape(1, d), mod]
    out_specs = [row_spec]
    out_shape = [jax.ShapeDtypeStruct((b, t, d), F32)]
    if nxt is not None:
        g_pre, nl, ns = nxt
        in_specs += [vec_spec, mod_spec(nl, ns), mod_spec(nl, ns + 1)]
        args += [g_pre.reshape(1, d), mod, mod]
        out_specs.append(row_spec)
        out_shape.append(jax.ShapeDtypeStruct((b, t, d), BF16))
    out = pl.pallas_call(
        functools.partial(_resid_kernel, fixed_row=fixed_row, with_next=nxt is not None),
        grid=(b, t // tr),
        in_specs=in_specs, out_specs=out_specs, out_shape=out_shape,
        compiler_params=_params(2),
        name="residual_norm",
    )(*args)
    return (out[0], out[1]) if nxt is not None else (out[0], None)


def _mm_kernel(*refs, n_parts, cast):
    a_refs = refs[:n_parts]
    w_ref, o_ref = refs[n_parts], refs[n_parts + 1]
    if cast:
        wbf = refs[n_parts + 2]

        @pl.when(pl.program_id(2) == 0)
        def _():
            wbf[...] = w_ref[...].astype(BF16)

        w_ref = wbf
    acc = None
    k0 = 0
    for a_ref in a_refs:
        kp = a_ref.shape[1]
        part = jnp.dot(a_ref[...], w_ref[k0:k0 + kp, :], preferred_element_type=F32)
        acc = part if acc is None else acc + part
        k0 += kp
    o_ref[...] = acc.astype(o_ref.dtype)


def _matmul(a_parts, w, out_dtype, *, w_index=None, tm=1024, tn=1024, name="matmul"):
    ga, m, _ = a_parts[0].shape
    gw, k, n = w.shape
    assert sum(a.shape[2] for a in a_parts) == k
    g = ga if w_index is not None else gw
    tm = _pick(m, tm)
    tn = _pick(n, tn)
    a_map = (lambda b, j, i: (b, i, 0)) if ga == g and ga > 1 else (lambda b, j, i: (0, i, 0))
    if w_index is not None:
        w_map = lambda b, j, i: (w_index, 0, j)
    else:
        w_map = lambda b, j, i: (b, 0, j)
    cast = w.dtype != BF16
    w_spec = (pl.BlockSpec((None, k, tn), w_map, pipeline_mode=pl.Buffered(1)) if cast
              else pl.BlockSpec((None, k, tn), w_map))
    return pl.pallas_call(
        functools.partial(_mm_kernel, n_parts=len(a_parts), cast=cast),
        grid=(g, n // tn, m // tm),
        in_specs=[pl.BlockSpec((None, tm, a.shape[2]), a_map) for a in a_parts] + [w_spec],
        out_specs=pl.BlockSpec((None, tm, tn), lambda b, j, i: (b, i, j)),
        out_shape=jax.ShapeDtypeStruct((g, m, n), out_dtype),
        scratch_shapes=[pltpu.VMEM((k, tn), BF16)] if cast else [],
        compiler_params=_params(3),
        name=name,
    )(*a_parts, w)


def _na_geometry(rows, rb):
    kh = min(NA_KH, rows)
    kr = min(rb + kh, rows)
    nblk = rows // rb
    slab = np.clip(np.arange(nblk) * rb - kh // 2, 0, rows - kr)
    variants, variant_of = [], []
    for j in range(nblk):
        geo = []
        for a in range(rb):
            r = j * rb + a
            rs = int(np.clip(r - kh // 2, 0, rows - kh))
            lo = rs - int(slab[j])
            geo.append((lo, lo + kh, rs - r + NA_KH - 1))
        geo = tuple(geo)
        if geo not in variants:
            variants.append(geo)
        variant_of.append(variants.index(geo))
    return kr, slab, np.asarray(variant_of), variants


def _na_bias(rpb, rows, rb):
    kr, slab, variant_of, variants = _na_geometry(rows, rb)
    h = rpb.shape[0]
    qcol = np.arange(GRID_W)
    cstart = np.clip(qcol - NA_KW // 2, 0, GRID_W - NA_KW)
    kcol = np.arange(GRID_W)
    col_ok = (kcol[None, :] >= cstart[:, None]) & (kcol[None, :] < cstart[:, None] + NA_KW)
    col_rel = np.clip(kcol[None, :] - qcol[:, None], -(NA_KW - 1), NA_KW - 1) + NA_KW - 1
    e = jnp.take(rpb, jnp.asarray(col_rel.reshape(-1)), axis=2).reshape(h, 2 * NA_KH - 1, GRID_W, GRID_W)
    e = jnp.where(jnp.asarray(col_ok)[None, None], e, NEG_INF)
    e = e.transpose(0, 2, 1, 3).reshape(h, GRID_W, (2 * NA_KH - 1) * GRID_W)
    out = []
    for geo in variants:
        rows_v = []
        for lo, hi, rr0 in geo:
            seg = e[:, :, rr0 * GRID_W:(rr0 + hi - lo) * GRID_W]
            rows_v.append(jnp.pad(seg, ((0, 0), (0, 0), (lo * GRID_W, (kr - hi) * GRID_W)),
                                  constant_values=NEG_INF))
        out.append(jnp.concatenate(rows_v, axis=1))
    return jnp.stack(out, axis=1).astype(F32), kr, slab, variant_of


def _nt_dot(a, b):
    return lax.dot_general(a, b, (((1,), (1,)), ((), ())), preferred_element_type=F32)


def _na_kernel(slab_ref, var_ref, q_ref, k_ref, v_ref, kc_ref, vc_ref, bias_ref, o_ref, *, kr, hp):
    del var_ref
    j = pl.program_id(2)
    ks = pl.multiple_of(slab_ref[j] * GRID_W, GRID_W)
    nk = kr * GRID_W
    hd = HEAD_DIM
    scale = hd ** -0.5
    for hh in range(hp):
        cols = slice(hh * hd, (hh + 1) * hd)
        q = q_ref[:, cols]
        s_loc = _nt_dot(q, k_ref[pl.ds(ks, nk), cols]) * scale + bias_ref[hh]
        s_ctx = _nt_dot(q, kc_ref[:, cols]) * scale
        m = jnp.maximum(jnp.max(s_loc, axis=-1, keepdims=True), jnp.max(s_ctx, axis=-1, keepdims=True))
        p_loc = jnp.exp(s_loc - m)
        p_ctx = jnp.exp(s_ctx - m)
        den = jnp.sum(p_loc, axis=-1, keepdims=True) + jnp.sum(p_ctx, axis=-1, keepdims=True)
        o = (jnp.dot(p_loc.astype(BF16), v_ref[pl.ds(ks, nk), cols], preferred_element_type=F32)
             + jnp.dot(p_ctx.astype(BF16), vc_ref[:, cols], preferred_element_type=F32))
        o_ref[:, cols] = (o / den).astype(o_ref.dtype)


def _neighborhood_attn(p, pc, rpb, n_heads):
    b, t, _ = p.shape
    l = pc.shape[1]
    rows = t // GRID_W
    rb = min(NA_ROW_BLOCK, rows)
    hp = min(NA_HEADS_PER_STEP, n_heads)
    assert n_heads % hp == 0
    ng = n_heads // hp
    bias, kr, slab, variant_of = _na_bias(rpb, rows, rb)
    nblk = rows // rb
    tq = rb * GRID_W
    w = hp * HEAD_DIM
    grid_spec = pltpu.PrefetchScalarGridSpec(
        num_scalar_prefetch=2,
        grid=(b, ng, nblk),
        in_specs=[
            pl.BlockSpec((None, tq, w), lambda i, h, j, s, v: (i, j, h)),
            pl.BlockSpec((None, t, w), lambda i, h, j, s, v: (i, 0, ng + h)),
            pl.BlockSpec((None, t, w), lambda i, h, j, s, v: (i, 0, 2 * ng + h)),
            pl.BlockSpec((None, l, w), lambda i, h, j, s, v: (i, 0, ng + h)),
            pl.BlockSpec((None, l, w), lambda i, h, j, s, v: (i, 0, 2 * ng + h)),
            pl.BlockSpec((hp, None, tq, kr * GRID_W), lambda i, h, j, s, v: (h, v[j], 0, 0)),
        ],
        out_specs=pl.BlockSpec((None, tq, w), lambda i, h, j, s, v: (i, j, h)),
    )
    return pl.pallas_call(
        functools.partial(_na_kernel, kr=kr, hp=hp),
        grid_spec=grid_spec,
        out_shape=jax.ShapeDtypeStruct((b, t, n_heads * HEAD_DIM), BF16),
        compiler_params=_params(3),
        name="neighborhood_attn",
    )(jnp.asarray(slab, jnp.int32), jnp.asarray(variant_of, jnp.int32), p, p, p, pc, pc, bias)


def _rope_tables(s):
    tpos = jnp.arange(s)
    row = (tpos // GRID_W).astype(F32)
    col = (tpos % GRID_W).astype(F32)
    nf = HEAD_DIM // 4
    inv = ROPE_THETA ** (-jnp.arange(nf, dtype=F32) / nf)
    lane = np.arange(HEAD_DIM)
    axis = lane // (2 * nf)
    second = (lane % (2 * nf)) >= nf
    freq = lane % nf
    pos = jnp.where(jnp.asarray(axis)[None, :] == 0, row[:, None], col[:, None])
    ang = pos * inv[jnp.asarray(freq)][None, :]
    cos, sin = jnp.cos(ang), jnp.sin(ang)
    sec = jnp.asarray(second)[None, :]
    return cos, jnp.where(sec, 0.0, -sin), jnp.where(sec, sin, 0.0)


def _rope(x, c, s1, s2):
    nf = HEAD_DIM // 4
    return x * c + pltpu.roll(x, HEAD_DIM - nf, axis=1) * s1 + pltpu.roll(x, nf, axis=1) * s2


def _wa_kernel(sink_ref, q_ref, k_ref, v_ref, kc_ref, vc_ref, c_ref, s1_ref, s2_ref, o_ref,
               *, group, kp, seq):
    kg = pl.program_id(1)
    n = pl.program_id(2)
    wb, hd = WA_BLOCK, HEAD_DIM
    nl = 3 * wb
    q0 = pl.multiple_of(n * wb, wb)
    bs = pl.multiple_of(jnp.clip(n * wb - wb, 0, seq - nl), wb)
    scale = hd ** -0.5
    cq, s1q, s2q = c_ref[pl.ds(q0, wb), :], s1_ref[pl.ds(q0, wb), :], s2_ref[pl.ds(q0, wb), :]
    ck, s1k, s2k = c_ref[pl.ds(bs, nl), :], s1_ref[pl.ds(bs, nl), :], s2_ref[pl.ds(bs, nl), :]
    rowi = lax.broadcasted_iota(jnp.int32, (group * wb, nl), 0)
    kpos = bs + lax.broadcasted_iota(jnp.int32, (group * wb, nl), 1)
    qpos = q0 + (rowi & (wb - 1))
    in_window = jnp.abs(kpos - qpos) <= WA_WINDOW
    gi = lax.broadcasted_iota(jnp.int32, (group * wb, 1), 0) // wb
    for kk in range(kp):
        kcols = slice(kk * hd, (kk + 1) * hd)
        q = jnp.concatenate(
            [_rope(q_ref[:, (kk * group + g) * hd:(kk * group + g + 1) * hd].astype(F32), cq, s1q, s2q)
             .astype(BF16) for g in range(group)], axis=0)
        kb = _rope(k_ref[pl.ds(bs, nl), kcols].astype(F32), ck, s1k, s2k).astype(BF16)
        s_loc = jnp.where(in_window, _nt_dot(q, kb) * scale, NEG_INF)
        s_ctx = _nt_dot(q, kc_ref[:, kcols]) * scale
        sink = jnp.zeros((group * wb, 1), F32)
        for g in range(group):
            sink = jnp.where(gi == g, sink_ref[(kg * kp + kk) * group + g], sink)
        m = jnp.maximum(jnp.maximum(jnp.max(s_loc, axis=-1, keepdims=True),
                                    jnp.max(s_ctx, axis=-1, keepdims=True)), sink)
        p_loc = jnp.exp(s_loc - m)
        p_ctx = jnp.exp(s_ctx - m)
        den = (jnp.sum(p_loc, axis=-1, keepdims=True) + jnp.sum(p_ctx, axis=-1, keepdims=True)
               + jnp.exp(sink - m))
        o = (jnp.dot(p_loc.astype(BF16), v_ref[pl.ds(bs, nl), kcols], preferred_element_type=F32)
             + jnp.dot(p_ctx.astype(BF16), vc_ref[:, kcols], preferred_element_type=F32))
        o = (o / den).astype(o_ref.dtype)
        for g in range(group):
            o_ref[:, (kk * group + g) * hd:(kk * group + g + 1) * hd] = o[g * wb:(g + 1) * wb, :]


def _window_gqa(p, pc, sink, q_col, k_col, v_col, n_heads, rope):
    b, t, _ = p.shape
    l = pc.shape[1]
    kvh = WA_KV_HEADS
    group = n_heads // kvh
    kp = min(WA_KV_PER_STEP, kvh)
    hd, wb = HEAD_DIM, WA_BLOCK
    qw = kp * group * hd
    kw = kp * hd
    assert kvh % kp == 0 and q_col % qw == 0 and k_col % kw == 0 and v_col % kw == 0
    assert t % wb == 0 and t >= 3 * wb
    cos, s1, s2 = rope
    tab_spec = pl.BlockSpec((t, hd), lambda i, k, n: (0, 0))
    return pl.pallas_call(
        functools.partial(_wa_kernel, group=group, kp=kp, seq=t),
        grid=(b, kvh // kp, t // wb),
        in_specs=[
            pl.BlockSpec(memory_space=pltpu.SMEM),
            pl.BlockSpec((None, wb, qw), lambda i, k, n: (i, n, q_col // qw + k)),
            pl.BlockSpec((None, t, kw), lambda i, k, n: (i, 0, k_col // kw + k)),
            pl.BlockSpec((None, t, kw), lambda i, k, n: (i, 0, v_col // kw + k)),
            pl.BlockSpec((None, l, kw), lambda i, k, n: (i, 0, k_col // kw + k)),
            pl.BlockSpec((None, l, kw), lambda i, k, n: (i, 0, v_col // kw + k)),
            tab_spec, tab_spec, tab_spec,
        ],
        out_specs=pl.BlockSpec((None, wb, qw), lambda i, k, n: (i, n, k)),
        out_shape=jax.ShapeDtypeStruct((b, t, n_heads * hd), BF16),
        compiler_params=_params(3),
        name="window_gqa",
    )(sink, p, p, p, pc, pc, cos, s1, s2)


def _ctx_kernel(sink_ref, q_ref, k_ref, v_ref, o_ref):
    h = pl.program_id(1)
    scale = HEAD_DIM ** -0.5
    s = _nt_dot(q_ref[...], k_ref[...]) * scale
    sink = sink_ref[h]
    m = jnp.maximum(jnp.max(s, axis=-1, keepdims=True), sink)
    p = jnp.exp(s - m)
    den = jnp.sum(p, axis=-1, keepdims=True) + jnp.exp(sink - m)
    o = jnp.dot(p.astype(BF16), v_ref[...], preferred_element_type=F32)
    o_ref[...] = (o / den).astype(o_ref.dtype)


def _ctx_attn(pc, sink, na_heads, wa_heads):
    b, l, _ = pc.shape
    hd = HEAD_DIM
    group = wa_heads // WA_KV_HEADS
    qb0 = 3 * na_heads
    kb0 = qb0 + wa_heads
    vb0 = kb0 + WA_KV_HEADS
    sink_all = jnp.concatenate([jnp.full((na_heads,), NEG_INF, F32), sink.astype(F32)])

    def qmap(i, h):
        return (i, 0, jnp.where(h < na_heads, h, qb0 + h - na_heads))

    def kmap(i, h):
        return (i, 0, jnp.where(h < na_heads, na_heads + h, kb0 + (h - na_heads) // group))

    def vmap(i, h):
        return (i, 0, jnp.where(h < na_heads, 2 * na_heads + h, vb0 + (h - na_heads) // group))

    return pl.pallas_call(
        _ctx_kernel,
        grid=(b, na_heads + wa_heads),
        in_specs=[pl.BlockSpec(memory_space=pltpu.SMEM),
                  pl.BlockSpec((None, l, hd), qmap),
                  pl.BlockSpec((None, l, hd), kmap),
                  pl.BlockSpec((None, l, hd), vmap)],
        out_specs=pl.BlockSpec((None, l, hd), lambda i, h: (i, 0, h)),
        out_shape=jax.ShapeDtypeStruct((b, l, (na_heads + wa_heads) * hd), BF16),
        compiler_params=_params(2),
        name="context_attn",
    )(sink_all, pc, pc, pc)


def _dft_cos_sin(n):
    lo = 1
    while lo * lo < n:
        lo *= 2
    hi = n // lo
    k = jnp.arange(n, dtype=jnp.int32)
    a = (2.0 * np.pi / n) * ((k[:, None] * (jnp.arange(hi, dtype=jnp.int32) * lo)[None, :]) % n).astype(F32)
    bb = (2.0 * np.pi / n) * ((k[:, None] * jnp.arange(lo, dtype=jnp.int32)[None, :]) % n).astype(F32)
    ca, sa, cb, sb = jnp.cos(a)[:, :, None], jnp.sin(a)[:, :, None], jnp.cos(bb)[:, None, :], jnp.sin(bb)[:, None, :]
    return (ca * cb - sa * sb).reshape(n, n), (sa * cb + ca * sb).reshape(n, n)


def _group_dft_kernel(u_ref, cs_ref, o_ref, *, groups):
    gd = FNET_GROUP_DIM
    cs = cs_ref[...]
    for g in range(groups):
        r = jnp.dot(u_ref[:, g * gd:(g + 1) * gd], cs, preferred_element_type=F32).astype(o_ref.dtype)
        o_ref[0, :, g * gd:(g + 1) * gd] = r[:, :gd]
        o_ref[1, :, g * gd:(g + 1) * gd] = r[:, gd:]


def _group_dft(p, width):
    b, t, _ = p.shape
    gd = FNET_GROUP_DIM
    c, s = _dft_cos_sin(gd)
    cs = (jnp.concatenate([c, s], axis=1) * (gd ** -0.5)).astype(BF16)
    tt = _pick(t, 512)
    return pl.pallas_call(
        functools.partial(_group_dft_kernel, groups=width // gd),
        grid=(b, t // tt),
        in_specs=[pl.BlockSpec((None, tt, width), lambda i, j: (i, j, 0)),
                  pl.BlockSpec((gd, 2 * gd), lambda i, j: (0, 0))],
        out_specs=pl.BlockSpec((None, 2, tt, width), lambda i, j: (i, 0, j, 0)),
        out_shape=jax.ShapeDtypeStruct((b, 2, t, width), BF16),
        compiler_params=_params(2),
        name="group_dft",
    )(p, cs)


def _seq_dft_matrix(t):
    c, s = _dft_cos_sin(t)
    return (jnp.concatenate([c, -s], axis=1) * (t ** -0.5)).astype(BF16)[None]


def _dwconv3(z, w_ref):
    t = z.shape[0]
    sub = min(F32_SUBLANES, t)
    row = lax.broadcasted_iota(jnp.int32, (sub, z.shape[1]), 0)
    zp = pltpu.roll(z, 1, axis=0)
    zn = pltpu.roll(z, t - 1, axis=0)
    zp = jnp.concatenate([jnp.where(row == 0, 0.0, zp[:sub]), zp[sub:]], axis=0)
    zn = jnp.concatenate([zn[:t - sub], jnp.where(row == sub - 1, 0.0, zn[t - sub:])], axis=0)
    return zp * w_ref[0:1, :] + z * w_ref[1:2, :] + zn * w_ref[2:3, :]


def _shortconv_kernel(bg_ref, cg_ref, hv_ref, w_ref, o_ref):
    z = cg_ref[...].astype(F32) * hv_ref[...].astype(F32)
    o_ref[...] = (bg_ref[...].astype(F32) * _dwconv3(z, w_ref)).astype(o_ref.dtype)


def _shortconv(p, conv_w, layer, col0, width):
    b, t, _ = p.shape
    tc = LANES
    nb = width // tc
    c0 = col0 // tc
    return pl.pallas_call(
        _shortconv_kernel,
        grid=(b, nb),
        in_specs=[pl.BlockSpec((None, t, tc), lambda i, j: (i, 0, c0 + j)),
                  pl.BlockSpec((None, t, tc), lambda i, j: (i, 0, c0 + nb + j)),
                  pl.BlockSpec((None, t, tc), lambda i, j: (i, 0, c0 + 2 * nb + j)),
                  pl.BlockSpec((None, 3, tc), lambda i, j: (layer, 0, j))],
        out_specs=pl.BlockSpec((None, t, tc), lambda i, j: (i, 0, j)),
        out_shape=jax.ShapeDtypeStruct((b, t, width), BF16),
        compiler_params=_params(2),
        name="shortconv_gate",
    )(p, p, p, conv_w)


def _ffn_gate_kernel(g_ref, v_ref, wg_ref, wv_ref, o_ref):
    gate = _dwconv3(g_ref[...].astype(F32), wg_ref)
    val = _dwconv3(v_ref[...].astype(F32), wv_ref)
    o_ref[...] = (gate / (1.0 + jnp.exp(-gate)) * val).astype(o_ref.dtype)


def _ffn_gate(u, conv_w, layer):
    b, t, f2 = u.shape
    f = f2 // 2
    tc = LANES
    nb = f // tc
    return pl.pallas_call(
        _ffn_gate_kernel,
        grid=(b, nb),
        in_specs=[pl.BlockSpec((None, t, tc), lambda i, j: (i, 0, j)),
                  pl.BlockSpec((None, t, tc), lambda i, j: (i, 0, nb + j)),
                  pl.BlockSpec((None, 3, tc), lambda i, j: (layer, 0, j)),
                  pl.BlockSpec((None, 3, tc), lambda i, j: (layer, 0, nb + j))],
        out_specs=pl.BlockSpec((None, t, tc), lambda i, j: (i, 0, j)),
        out_shape=jax.ShapeDtypeStruct((b, t, f), BF16),
        compiler_params=_params(2),
        name="ffn_conv_gate",
    )(u, u, conv_w, conv_w)


def _proj(parts, w, layer, out_dtype, **kw):
    b, t, _ = parts[0].shape
    y = _matmul([a.reshape(1, b * t, a.shape[2]) for a in parts], w, out_dtype, w_index=layer, **kw)
    return y.reshape(b, t, -1)


def _even_mixer(hl, hc, w_in, rpb, sink, w_out, e, rope, ctx_live):
    na_heads = rpb.shape[1]
    wa_heads = sink.shape[1]
    na_w = na_heads * HEAD_DIM
    wa_w = wa_heads * HEAD_DIM
    kv_w = WA_KV_HEADS * HEAD_DIM
    p = _proj([hl], w_in, e, BF16, name="even_w_in")
    pc = _proj([hc], w_in, e, BF16, name="even_w_in_ctx")
    oa = _neighborhood_attn(p, pc, rpb[e], na_heads)
    ob = _window_gqa(p, pc, sink[e].astype(F32), 3 * na_w, 3 * na_w + wa_w, 3 * na_w + wa_w + kv_w,
                     wa_heads, rope)
    yl = _proj([oa, ob], w_out, e, F32, name="even_w_out")
    yc = None
    if ctx_live:
        yc = _proj([_ctx_attn(pc, sink[e], na_heads, wa_heads)], w_out, e, F32, name="even_w_out_ctx")
    return yl, yc


def _odd_mixer(h, w_in, conv_w, w_out, o):
    b, t, _ = h.shape
    sc_w = conv_w.shape[2]
    f_w = w_in.shape[2] - 3 * sc_w
    p = _proj([h], w_in, o, BF16, name="odd_w_in")
    v = _group_dft(p, f_w).reshape(b, 2 * t, f_w)
    f = _matmul([_seq_dft_matrix(t)], v, BF16, tm=512, tn=512, name="seq_dft")
    sc = _shortconv(p, conv_w, o, f_w, sc_w)
    return _proj([f, sc], w_out, o, F32, name="odd_w_out")


def _conv_ffn(h, w_up, conv_w, w_down, i):
    u = _proj([h], w_up, i, BF16, name="ffn_w_up")
    a = _ffn_gate(u, conv_w, i)
    return _proj([a], w_down, i, F32, tm=512, tn=1024, name="ffn_w_down")


def kernel(x, c, ctx, c_ctx, w_mod, b_mod, g_mix_pre, g_mix_post, g_ffn_pre, g_ffn_post, even_w_in, even_rpb, even_sink, even_w_out, odd_w_in, odd_conv, odd_w_out, ffn_w_up, ffn_conv, ffn_w_down):
    b, t, d = x.shape
    depth = w_mod.shape[0]
    rows = -(-(b + 1) // 8) * 8
    cond = jnp.zeros((rows, d), F32).at[:b].set(c).at[b].set(c_ctx)
    mod = _modulation(cond, w_mod, b_mod)
    rope = _rope_tables(t)
    xc = ctx
    hl = _prenorm(x, g_mix_pre[0], mod, 0, 0)
    for i in range(depth):
        ctx_live = any(j % 2 == 0 for j in range(i + 1, depth))
        need_hc = (i % 2 == 0) or ctx_live
        hc = _prenorm(xc, g_mix_pre[i], mod, i, 0, fixed_row=b) if need_hc else None
        if i % 2 == 0:
            e = i // 2
            yl, yc = _even_mixer(hl, hc, even_w_in, even_rpb, even_sink, even_w_out, e, rope, ctx_live)
        else:
            o = i // 2
            yl = _odd_mixer(hl, odd_w_in, odd_conv, odd_w_out, o)
            yc = _odd_mixer(hc, odd_w_in, odd_conv, odd_w_out, o) if ctx_live else None
        x, hl = _residual(x, yl, g_mix_post[i], mod, i, 2, nxt=(g_ffn_pre[i], i, 3))
        yf = _conv_ffn(hl, ffn_w_up, ffn_conv, ffn_w_down, i)
        nxt = (g_mix_pre[i + 1], i + 1, 0) if i + 1 < depth else None
        x, hl = _residual(x, yf, g_ffn_post[i], mod, i, 5, nxt=nxt)
        if ctx_live:
            xc, hcf = _residual(xc, yc, g_mix_post[i], mod, i, 2, nxt=(g_ffn_pre[i], i, 3), fixed_row=b)
            yfc = _conv_ffn(hcf, ffn_w_up, ffn_conv, ffn_w_down, i)
            xc, _ = _residual(xc, yfc, g_ffn_post[i], mod, i, 5, fixed_row=b)
    return x
```

```python
import functools

import numpy as np
import jax
import jax.numpy as jnp
from jax import lax
from jax.experimental import pallas as pl
from jax.experimental.pallas import tpu as pltpu

F32 = jnp.float32
BF16 = jnp.bfloat16

GRID_W = 64
HEAD_DIM = 128
NA_KH = 8
NA_KW = 16
WA_KV_HEADS = 4
WA_WINDOW = 128
WA_BLOCK = 128
FNET_GROUP_DIM = 128
ROPE_THETA = 10000.0
RMS_EPS = 1e-6
NEG_INF = -1e30

V7X_VMEM_BYTES = 64 * 1024 * 1024
VMEM_LIMIT = V7X_VMEM_BYTES - 8 * 1024 * 1024
LANES = 128

NA_ROW_BLOCK = 4
NA_HEADS_PER_STEP = 4
WA_KV_PER_STEP = 2
F32_SUBLANES = 8
MIN_ROW_TILES_FOR_STAGING = 8


def _params(n_grid):
    return pltpu.CompilerParams(dimension_semantics=("arbitrary",) * n_grid,
                                vmem_limit_bytes=VMEM_LIMIT)


def _pick(n, pref):
    if n <= pref:
        return n
    t = pref
    while n % t:
        t //= 2
    return t


def _mod_kernel(c_ref, w_ref, b_ref, o_ref):
    c = c_ref[...]
    s = (c / (1.0 + jnp.exp(-c))).astype(BF16)
    y = jnp.dot(s, w_ref[...].astype(BF16), preferred_element_type=F32)
    o_ref[...] = y + b_ref[...]


def _modulation(cond, w_mod, b_mod):
    depth, d, n = w_mod.shape
    rows = cond.shape[0]
    tn = _pick(n, 512)
    return pl.pallas_call(
        _mod_kernel,
        grid=(depth, n // tn),
        in_specs=[pl.BlockSpec((rows, d), lambda l, j: (0, 0)),
                  pl.BlockSpec((None, d, tn), lambda l, j: (l, 0, j)),
                  pl.BlockSpec((None, 1, tn), lambda l, j: (l, 0, j))],
        out_specs=pl.BlockSpec((None, rows, tn), lambda l, j: (l, 0, j)),
        out_shape=jax.ShapeDtypeStruct((depth, rows, n), F32),
        compiler_params=_params(2),
        name="adaln_mod",
    )(cond, w_mod, b_mod.reshape(depth, 1, n))


def _rms(v, g):
    return v * lax.rsqrt(jnp.mean(v * v, axis=-1, keepdims=True) + RMS_EPS) * g


def _mod_row(ref, fixed_row):
    row = pl.program_id(0) if fixed_row is None else fixed_row
    return ref[pl.ds(row, 1), :]


def _prenorm_kernel(x_ref, g_ref, sh_ref, sc_ref, h_ref, *, fixed_row):
    sh = _mod_row(sh_ref, fixed_row)
    sc = _mod_row(sc_ref, fixed_row)
    h_ref[...] = (_rms(x_ref[...], g_ref[...]) * (1.0 + sc) + sh).astype(h_ref.dtype)


def _prenorm(x, g, mod, layer, slot, fixed_row=None):
    b, t, d = x.shape
    rows = mod.shape[1]
    tr = _pick(t, 256)
    return pl.pallas_call(
        functools.partial(_prenorm_kernel, fixed_row=fixed_row),
        grid=(b, t // tr),
        in_specs=[pl.BlockSpec((None, tr, d), lambda i, j: (i, j, 0)),
                  pl.BlockSpec((1, d), lambda i, j: (0, 0)),
                  pl.BlockSpec((None, rows, d), lambda i, j: (layer, 0, slot)),
                  pl.BlockSpec((None, rows, d), lambda i, j: (layer, 0, slot + 1))],
        out_specs=pl.BlockSpec((None, tr, d), lambda i, j: (i, j, 0)),
        out_shape=jax.ShapeDtypeStruct((b, t, d), BF16),
        compiler_params=_params(2),
        name="prenorm",
    )(x, g.reshape(1, d), mod, mod)


def _resid_kernel(x_ref, y_ref, gp_ref, gt_ref, *rest, fixed_row, with_next):
    gt = _mod_row(gt_ref, fixed_row)
    xn = x_ref[...] + gt * _rms(y_ref[...].astype(F32), gp_ref[...])
    if with_next:
        gn_ref, sh_ref, sc_ref, xo_ref, h_ref = rest
        sh = _mod_row(sh_ref, fixed_row)
        sc = _mod_row(sc_ref, fixed_row)
        h_ref[...] = (_rms(xn, gn_ref[...]) * (1.0 + sc) + sh).astype(h_ref.dtype)
    else:
        (xo_ref,) = rest
    xo_ref[...] = xn


def _residual(x, y, g_post, mod, layer, gate_slot, nxt=None, fixed_row=None):
    b, t, d = x.shape
    rows = mod.shape[1]
    tr = _pick(t, 256)
    row_spec = pl.BlockSpec((None, tr, d), lambda i, j: (i, j, 0))
    vec_spec = pl.BlockSpec((1, d), lambda i, j: (0, 0))

    def mod_spec(l, s):
        return pl.BlockSpec((None, rows, d), lambda i, j: (l, 0, s))

    in_specs = [row_spec, row_spec, vec_spec, mod_spec(layer, gate_slot)]
    args = [x, y, g_post.reshape(1, d), mod]
    out_specs = [row_spec]
    out_shape = [jax.ShapeDtypeStruct((b, t, d), F32)]
    if nxt is not None:
        g_pre, nl, ns = nxt
        in_specs += [vec_spec, mod_spec(nl, ns), mod_spec(nl, ns + 1)]
        args += [g_pre.reshape(1, d), mod, mod]
        out_specs.append(row_spec)
        out_shape.append(jax.ShapeDtypeStruct((b, t, d), BF16))
    out = pl.pallas_call(
        functools.partial(_resid_kernel, fixed_row=fixed_row, with_next=nxt is not None),
        grid=(b, t // tr),
        in_specs=in_specs, out_specs=out_specs, out_shape=out_shape,
        compiler_params=_params(2),
        name="residual_norm",
    )(*args)
    return (out[0], out[1]) if nxt is not None else (out[0], None)


def _mm_kernel(*refs, n_parts, cast):
    a_refs = refs[:n_parts]
    w_ref, o_ref = refs[n_parts], refs[n_parts + 1]
    if cast:
        wbf = refs[n_parts + 2]

        @pl.when(pl.program_id(2) == 0)
        def _():
            wbf[...] = w_ref[...].astype(BF16)

        w_ref = wbf
    acc = None
    k0 = 0
    for a_ref in a_refs:
        kp = a_ref.shape[1]
        part = jnp.dot(a_ref[...], w_ref[k0:k0 + kp, :], preferred_element_type=F32)
        acc = part if acc is None else acc + part
        k0 += kp
    o_ref[...] = acc.astype(o_ref.dtype)


def _matmul(a_parts, w, out_dtype, *, w_index=None, tm=1024, tn=1024, name="matmul"):
    ga, m, _ = a_parts[0].shape
    gw, k, n = w.shape
    assert sum(a.shape[2] for a in a_parts) == k
    g = ga if w_index is not None else gw
    tm = _pick(m, tm)
    cast = w.dtype != BF16
    stage_once = cast and m // tm >= MIN_ROW_TILES_FOR_STAGING
    tn = _pick(n, tn if stage_once or not cast else tn // 2)
    a_map = (lambda b, j, i: (b, i, 0)) if ga == g and ga > 1 else (lambda b, j, i: (0, i, 0))
    if w_index is not None:
        w_map = lambda b, j, i: (w_index, 0, j)
    else:
        w_map = lambda b, j, i: (b, 0, j)
    w_spec = (pl.BlockSpec((None, k, tn), w_map, pipeline_mode=pl.Buffered(1)) if stage_once
              else pl.BlockSpec((None, k, tn), w_map))
    return pl.pallas_call(
        functools.partial(_mm_kernel, n_parts=len(a_parts), cast=cast),
        grid=(g, n // tn, m // tm),
        in_specs=[pl.BlockSpec((None, tm, a.shape[2]), a_map) for a in a_parts] + [w_spec],
        out_specs=pl.BlockSpec((None, tm, tn), lambda b, j, i: (b, i, j)),
        out_shape=jax.ShapeDtypeStruct((g, m, n), out_dtype),
        scratch_shapes=[pltpu.VMEM((k, tn), BF16)] if cast else [],
        compiler_params=_params(3),
        name=name,
    )(*a_parts, w)


def _na_geometry(rows, rb):
    kh = min(NA_KH, rows)
    kr = min(rb + kh, rows)
    nblk = rows // rb
    slab = np.clip(np.arange(nblk) * rb - kh // 2, 0, rows - kr)
    variants, variant_of = [], []
    for j in range(nblk):
        geo = []
        for a in range(rb):
            r = j * rb + a
            rs = int(np.clip(r - kh // 2, 0, rows - kh))
            lo = rs - int(slab[j])
            geo.append((lo, lo + kh, rs - r + NA_KH - 1))
        geo = tuple(geo)
        if geo not in variants:
            variants.append(geo)
        variant_of.append(variants.index(geo))
    return kr, slab, np.asarray(variant_of), variants


def _na_bias(rpb, rows, rb):
    kr, slab, variant_of, variants = _na_geometry(rows, rb)
    h = rpb.shape[0]
    qcol = np.arange(GRID_W)
    cstart = np.clip(qcol - NA_KW // 2, 0, GRID_W - NA_KW)
    kcol = np.arange(GRID_W)
    col_ok = (kcol[None, :] >= cstart[:, None]) & (kcol[None, :] < cstart[:, None] + NA_KW)
    col_rel = np.clip(kcol[None, :] - qcol[:, None], -(NA_KW - 1), NA_KW - 1) + NA_KW - 1
    e = jnp.take(rpb, jnp.asarray(col_rel.reshape(-1)), axis=2).reshape(h, 2 * NA_KH - 1, GRID_W, GRID_W)
    e = jnp.where(jnp.asarray(col_ok)[None, None], e, NEG_INF)
    e = e.transpose(0, 2, 1, 3).reshape(h, GRID_W, (2 * NA_KH - 1) * GRID_W)
    out = []
    for geo in variants:
        rows_v = []
        for lo, hi, rr0 in geo:
            seg = e[:, :, rr0 * GRID_W:(rr0 + hi - lo) * GRID_W]
            rows_v.append(jnp.pad(seg, ((0, 0), (0, 0), (lo * GRID_W, (kr - hi) * GRID_W)),
                                  constant_values=NEG_INF))
        out.append(jnp.concatenate(rows_v, axis=1))
    return jnp.stack(out, axis=1).astype(F32), kr, slab, variant_of


def _nt_dot(a, b):
    return lax.dot_general(a, b, (((1,), (1,)), ((), ())), preferred_element_type=F32)


def _na_kernel(slab_ref, var_ref, q_ref, k_ref, v_ref, kc_ref, vc_ref, bias_ref, o_ref, *, kr, hp):
    del var_ref
    j = pl.program_id(2)
    ks = pl.multiple_of(slab_ref[j] * GRID_W, GRID_W)
    nk = kr * GRID_W
    hd = HEAD_DIM
    scale = hd ** -0.5
    for hh in range(hp):
        cols = slice(hh * hd, (hh + 1) * hd)
        q = q_ref[:, cols]
        s_loc = _nt_dot(q, k_ref[pl.ds(ks, nk), cols]) * scale + bias_ref[hh]
        s_ctx = _nt_dot(q, kc_ref[:, cols]) * scale
        m = jnp.maximum(jnp.max(s_loc, axis=-1, keepdims=True), jnp.max(s_ctx, axis=-1, keepdims=True))
        p_loc = jnp.exp(s_loc - m)
        p_ctx = jnp.exp(s_ctx - m)
        den = jnp.sum(p_loc, axis=-1, keepdims=True) + jnp.sum(p_ctx, axis=-1, keepdims=True)
        o = (jnp.dot(p_loc.astype(BF16), v_ref[pl.ds(ks, nk), cols], preferred_element_type=F32)
             + jnp.dot(p_ctx.astype(BF16), vc_ref[:, cols], preferred_element_type=F32))
        o_ref[:, cols] = (o / den).astype(o_ref.dtype)


def _neighborhood_attn(p, pc, rpb, n_heads):
    b, t, _ = p.shape
    l = pc.shape[1]
    rows = t // GRID_W
    rb = min(NA_ROW_BLOCK, rows)
    hp = min(NA_HEADS_PER_STEP, n_heads)
    assert n_heads % hp == 0
    ng = n_heads // hp
    bias, kr, slab, variant_of = _na_bias(rpb, rows, rb)
    nblk = rows // rb
    tq = rb * GRID_W
    w = hp * HEAD_DIM
    grid_spec = pltpu.PrefetchScalarGridSpec(
        num_scalar_prefetch=2,
        grid=(b, ng, nblk),
        in_specs=[
            pl.BlockSpec((None, tq, w), lambda i, h, j, s, v: (i, j, h)),
            pl.BlockSpec((None, t, w), lambda i, h, j, s, v: (i, 0, ng + h)),
            pl.BlockSpec((None, t, w), lambda i, h, j, s, v: (i, 0, 2 * ng + h)),
            pl.BlockSpec((None, l, w), lambda i, h, j, s, v: (i, 0, ng + h)),
            pl.BlockSpec((None, l, w), lambda i, h, j, s, v: (i, 0, 2 * ng + h)),
            pl.BlockSpec((hp, None, tq, kr * GRID_W), lambda i, h, j, s, v: (h, v[j], 0, 0)),
        ],
        out_specs=pl.BlockSpec((None, tq, w), lambda i, h, j, s, v: (i, j, h)),
    )
    return pl.pallas_call(
        functools.partial(_na_kernel, kr=kr, hp=hp),
        grid_spec=grid_spec,
        out_shape=jax.ShapeDtypeStruct((b, t, n_heads * HEAD_DIM), BF16),
        compiler_params=_params(3),
        name="neighborhood_attn",
    )(jnp.asarray(slab, jnp.int32), jnp.asarray(variant_of, jnp.int32), p, p, p, pc, pc, bias)


def _rope_tables(s):
    tpos = jnp.arange(s)
    row = (tpos // GRID_W).astype(F32)
    col = (tpos % GRID_W).astype(F32)
    nf = HEAD_DIM // 4
    inv = ROPE_THETA ** (-jnp.arange(nf, dtype=F32) / nf)
    lane = np.arange(HEAD_DIM)
    axis = lane // (2 * nf)
    second = (lane % (2 * nf)) >= nf
    freq = lane % nf
    pos = jnp.where(jnp.asarray(axis)[None, :] == 0, row[:, None], col[:, None])
    ang = pos * inv[jnp.asarray(freq)][None, :]
    cos, sin = jnp.cos(ang), jnp.sin(ang)
    sec = jnp.asarray(second)[None, :]
    return cos, jnp.where(sec, 0.0, -sin), jnp.where(sec, sin, 0.0)


def _rope(x, c, s1, s2):
    nf = HEAD_DIM // 4
    return x * c + pltpu.roll(x, HEAD_DIM - nf, axis=1) * s1 + pltpu.roll(x, nf, axis=1) * s2


def _wa_kernel(sink_ref, q_ref, k_ref, v_ref, kc_ref, vc_ref, c_ref, s1_ref, s2_ref, o_ref,
               *, group, kp, seq):
    kg = pl.program_id(1)
    n = pl.program_id(2)
    wb, hd = WA_BLOCK, HEAD_DIM
    nl = 3 * wb
    q0 = pl.multiple_of(n * wb, wb)
    bs = pl.multiple_of(jnp.clip(n * wb - wb, 0, seq - nl), wb)
    scale = hd ** -0.5
    cq, s1q, s2q = c_ref[pl.ds(q0, wb), :], s1_ref[pl.ds(q0, wb), :], s2_ref[pl.ds(q0, wb), :]
    ck, s1k, s2k = c_ref[pl.ds(bs, nl), :], s1_ref[pl.ds(bs, nl), :], s2_ref[pl.ds(bs, nl), :]
    rowi = lax.broadcasted_iota(jnp.int32, (group * wb, nl), 0)
    kpos = bs + lax.broadcasted_iota(jnp.int32, (group * wb, nl), 1)
    qpos = q0 + (rowi & (wb - 1))
    in_window = jnp.abs(kpos - qpos) <= WA_WINDOW
    gi = lax.broadcasted_iota(jnp.int32, (group * wb, 1), 0) // wb
    for kk in range(kp):
        kcols = slice(kk * hd, (kk + 1) * hd)
        q = jnp.concatenate(
            [_rope(q_ref[:, (kk * group + g) * hd:(kk * group + g + 1) * hd].astype(F32), cq, s1q, s2q)
             .astype(BF16) for g in range(group)], axis=0)
        kb = _rope(k_ref[pl.ds(bs, nl), kcols].astype(F32), ck, s1k, s2k).astype(BF16)
        s_loc = jnp.where(in_window, _nt_dot(q, kb) * scale, NEG_INF)
        s_ctx = _nt_dot(q, kc_ref[:, kcols]) * scale
        sink = jnp.zeros((group * wb, 1), F32)
        for g in range(group):
            sink = jnp.where(gi == g, sink_ref[(kg * kp + kk) * group + g], sink)
        m = jnp.maximum(jnp.maximum(jnp.max(s_loc, axis=-1, keepdims=True),
                                    jnp.max(s_ctx, axis=-1, keepdims=True)), sink)
        p_loc = jnp.exp(s_loc - m)
        p_ctx = jnp.exp(s_ctx - m)
        den = (jnp.sum(p_loc, axis=-1, keepdims=True) + jnp.sum(p_ctx, axis=-1, keepdims=True)
               + jnp.exp(sink - m))
        o = (jnp.dot(p_loc.astype(BF16), v_ref[pl.ds(bs, nl), kcols], preferred_element_type=F32)
             + jnp.dot(p_ctx.astype(BF16), vc_ref[:, kcols], preferred_element_type=F32))
        o = (o / den).astype(o_ref.dtype)
        for g in range(group):
            o_ref[:, (kk * group + g) * hd:(kk * group + g + 1) * hd] = o[g * wb:(g + 1) * wb, :]


def _window_gqa(p, pc, sink, q_col, k_col, v_col, n_heads, rope):
    b, t, _ = p.shape
    l = pc.shape[1]
    kvh = WA_KV_HEADS
    group = n_heads // kvh
    kp = min(WA_KV_PER_STEP, kvh)
    hd, wb = HEAD_DIM, WA_BLOCK
    qw = kp * group * hd
    kw = kp * hd
    assert kvh % kp == 0 and q_col % qw == 0 and k_col % kw == 0 and v_col % kw == 0
    assert t % wb == 0 and t >= 3 * wb
    cos, s1, s2 = rope
    tab_spec = pl.BlockSpec((t, hd), lambda i, k, n: (0, 0))
    return pl.pallas_call(
        functools.partial(_wa_kernel, group=group, kp=kp, seq=t),
        grid=(b, kvh // kp, t // wb),
        in_specs=[
            pl.BlockSpec(memory_space=pltpu.SMEM),
            pl.BlockSpec((None, wb, qw), lambda i, k, n: (i, n, q_col // qw + k)),
            pl.BlockSpec((None, t, kw), lambda i, k, n: (i, 0, k_col // kw + k)),
            pl.BlockSpec((None, t, kw), lambda i, k, n: (i, 0, v_col // kw + k)),
            pl.BlockSpec((None, l, kw), lambda i, k, n: (i, 0, k_col // kw + k)),
            pl.BlockSpec((None, l, kw), lambda i, k, n: (i, 0, v_col // kw + k)),
            tab_spec, tab_spec, tab_spec,
        ],
        out_specs=pl.BlockSpec((None, wb, qw), lambda i, k, n: (i, n, k)),
        out_shape=jax.ShapeDtypeStruct((b, t, n_heads * hd), BF16),
        compiler_params=_params(3),
        name="window_gqa",
    )(sink, p, p, p, pc, pc, cos, s1, s2)


def _ctx_kernel(sink_ref, q_ref, k_ref, v_ref, o_ref):
    h = pl.program_id(1)
    scale = HEAD_DIM ** -0.5
    s = _nt_dot(q_ref[...], k_ref[...]) * scale
    sink = sink_ref[h]
    m = jnp.maximum(jnp.max(s, axis=-1, keepdims=True), sink)
    p = jnp.exp(s - m)
    den = jnp.sum(p, axis=-1, keepdims=True) + jnp.exp(sink - m)
    o = jnp.dot(p.astype(BF16), v_ref[...], preferred_element_type=F32)
    o_ref[...] = (o / den).astype(o_ref.dtype)


def _ctx_attn(pc, sink, na_heads, wa_heads):
    b, l, _ = pc.shape
    hd = HEAD_DIM
    group = wa_heads // WA_KV_HEADS
    qb0 = 3 * na_heads
    kb0 = qb0 + wa_heads
    vb0 = kb0 + WA_KV_HEADS
    sink_all = jnp.concatenate([jnp.full((na_heads,), NEG_INF, F32), sink.astype(F32)])

    def qmap(i, h):
        return (i, 0, jnp.where(h < na_heads, h, qb0 + h - na_heads))

    def kmap(i, h):
        return (i, 0, jnp.where(h < na_heads, na_heads + h, kb0 + (h - na_heads) // group))

    def vmap(i, h):
        return (i, 0, jnp.where(h < na_heads, 2 * na_heads + h, vb0 + (h - na_heads) // group))

    return pl.pallas_call(
        _ctx_kernel,
        grid=(b, na_heads + wa_heads),
        in_specs=[pl.BlockSpec(memory_space=pltpu.SMEM),
                  pl.BlockSpec((None, l, hd), qmap),
                  pl.BlockSpec((None, l, hd), kmap),
                  pl.BlockSpec((None, l, hd), vmap)],
        out_specs=pl.BlockSpec((None, l, hd), lambda i, h: (i, 0, h)),
        out_shape=jax.ShapeDtypeStruct((b, l, (na_heads + wa_heads) * hd), BF16),
        compiler_params=_params(2),
        name="context_attn",
    )(sink_all, pc, pc, pc)


def _dft_cos_sin(n, ks, nt):
    lo = 1
    while lo * lo < nt:
        lo *= 2
    hi = nt // lo
    k = jnp.asarray(ks, dtype=jnp.int32)
    a = (2.0 * np.pi / n) * ((k[:, None] * (jnp.arange(hi, dtype=jnp.int32) * lo)[None, :]) % n).astype(F32)
    bb = (2.0 * np.pi / n) * ((k[:, None] * jnp.arange(lo, dtype=jnp.int32)[None, :]) % n).astype(F32)
    ca, sa, cb, sb = jnp.cos(a)[:, :, None], jnp.sin(a)[:, :, None], jnp.cos(bb)[:, None, :], jnp.sin(bb)[:, None, :]
    return (ca * cb - sa * sb).reshape(len(ks), nt), (sa * cb + ca * sb).reshape(len(ks), nt)


def _group_dft_kernel(ua_ref, ub_ref, cs_ref, o_ref, *, groups):
    gd = FNET_GROUP_DIM
    cs = cs_ref[...]
    for g in range(groups):
        cols = slice(g * gd, (g + 1) * gd)
        ra = jnp.dot(ua_ref[:, cols], cs, preferred_element_type=F32)
        rb = jnp.dot(ub_ref[:, cols], cs, preferred_element_type=F32)
        plus = (ra + rb).astype(o_ref.dtype)
        minus = (ra - rb).astype(o_ref.dtype)
        o_ref[0, 0, :, cols] = plus[:, :gd]
        o_ref[0, 1, :, cols] = plus[:, gd:]
        o_ref[1, 0, :, cols] = minus[:, :gd]
        o_ref[1, 1, :, cols] = minus[:, gd:]


def _group_dft(p, width):
    b, t, _ = p.shape
    gd = FNET_GROUP_DIM
    c, s = _dft_cos_sin(gd, np.arange(gd), gd)
    cs = (jnp.concatenate([c, s], axis=1) * (gd ** -0.5)).astype(BF16)
    th = t // 2
    tt = _pick(th, 512)
    nh = th // tt
    return pl.pallas_call(
        functools.partial(_group_dft_kernel, groups=width // gd),
        grid=(b, nh),
        in_specs=[pl.BlockSpec((None, tt, width), lambda i, j: (i, j, 0)),
                  pl.BlockSpec((None, tt, width), lambda i, j: (i, nh + j, 0)),
                  pl.BlockSpec((gd, 2 * gd), lambda i, j: (0, 0))],
        out_specs=pl.BlockSpec((None, 2, 2, tt, width), lambda i, j: (i, 0, 0, j, 0)),
        out_shape=jax.ShapeDtypeStruct((b, 2, 2, th, width), BF16),
        compiler_params=_params(2),
        name="group_dft",
    )(p, p, cs)


def _seq_dft(v):
    b, _, _, th, w = v.shape
    t = 2 * th
    mats = []
    for parity in range(2):
        c, s = _dft_cos_sin(t, np.arange(parity, t, 2), th)
        mats.append(jnp.concatenate([c, -s], axis=1) * (t ** -0.5))
    a = jnp.stack(mats).astype(BF16)
    tm = _pick(th, 1024)
    tn = _pick(w, 1024)
    nj = w // tn
    out = pl.pallas_call(
        functools.partial(_mm_kernel, n_parts=1, cast=False),
        grid=(b, 2, nj, th // tm),
        in_specs=[pl.BlockSpec((None, tm, t), lambda g, q, j, i: (q, i, 0)),
                  pl.BlockSpec((None, None, t, tn), lambda g, q, j, i: (g, q, 0, j))],
        out_specs=pl.BlockSpec((None, tm, tn), lambda g, q, j, i: (g, i, q * nj + j)),
        out_shape=jax.ShapeDtypeStruct((b, th, 2 * w), BF16),
        compiler_params=_params(4),
        name="seq_dft",
    )(a, v.reshape(b, 2, t, w))
    return out.reshape(b, t, w)


def _dwconv3(z, w_ref):
    t = z.shape[0]
    sub = min(F32_SUBLANES, t)
    row = lax.broadcasted_iota(jnp.int32, (sub, z.shape[1]), 0)
    zp = pltpu.roll(z, 1, axis=0)
    zn = pltpu.roll(z, t - 1, axis=0)
    zp = jnp.concatenate([jnp.where(row == 0, 0.0, zp[:sub]), zp[sub:]], axis=0)
    zn = jnp.concatenate([zn[:t - sub], jnp.where(row == sub - 1, 0.0, zn[t - sub:])], axis=0)
    return zp * w_ref[0:1, :] + z * w_ref[1:2, :] + zn * w_ref[2:3, :]


def _shortconv_kernel(bg_ref, cg_ref, hv_ref, w_ref, o_ref):
    z = cg_ref[...].astype(F32) * hv_ref[...].astype(F32)
    o_ref[...] = (bg_ref[...].astype(F32) * _dwconv3(z, w_ref)).astype(o_ref.dtype)


def _shortconv(p, conv_w, layer, col0, width):
    b, t, _ = p.shape
    tc = LANES
    nb = width // tc
    c0 = col0 // tc
    return pl.pallas_call(
        _shortconv_kernel,
        grid=(b, nb),
        in_specs=[pl.BlockSpec((None, t, tc), lambda i, j: (i, 0, c0 + j)),
                  pl.BlockSpec((None, t, tc), lambda i, j: (i, 0, c0 + nb + j)),
                  pl.BlockSpec((None, t, tc), lambda i, j: (i, 0, c0 + 2 * nb + j)),
                  pl.BlockSpec((None, 3, tc), lambda i, j: (layer, 0, j))],
        out_specs=pl.BlockSpec((None, t, tc), lambda i, j: (i, 0, j)),
        out_shape=jax.ShapeDtypeStruct((b, t, width), BF16),
        compiler_params=_params(2),
        name="shortconv_gate",
    )(p, p, p, conv_w)


def _ffn_gate_kernel(g_ref, v_ref, wg_ref, wv_ref, o_ref):
    gate = _dwconv3(g_ref[...].astype(F32), wg_ref)
    val = _dwconv3(v_ref[...].astype(F32), wv_ref)
    o_ref[...] = (gate / (1.0 + jnp.exp(-gate)) * val).astype(o_ref.dtype)


def _ffn_gate(u, conv_w, layer):
    b, t, f2 = u.shape
    f = f2 // 2
    tc = LANES
    nb = f // tc
    return pl.pallas_call(
        _ffn_gate_kernel,
        grid=(b, nb),
        in_specs=[pl.BlockSpec((None, t, tc), lambda i, j: (i, 0, j)),
                  pl.BlockSpec((None, t, tc), lambda i, j: (i, 0, nb + j)),
                  pl.BlockSpec((None, 3, tc), lambda i, j: (layer, 0, j)),
                  pl.BlockSpec((None, 3, tc), lambda i, j: (layer, 0, nb + j))],
        out_specs=pl.BlockSpec((None, t, tc), lambda i, j: (i, 0, j)),
        out_shape=jax.ShapeDtypeStruct((b, t, f), BF16),
        compiler_params=_params(2),
        name="ffn_conv_gate",
    )(u, u, conv_w, conv_w)


def _proj(parts, w, layer, out_dtype, **kw):
    b, t, _ = parts[0].shape
    y = _matmul([a.reshape(1, b * t, a.shape[2]) for a in parts], w, out_dtype, w_index=layer, **kw)
    return y.reshape(b, t, -1)


def _even_mixer(hl, hc, w_in, rpb, sink, w_out, e, rope, ctx_live):
    na_heads = rpb.shape[1]
    wa_heads = sink.shape[1]
    na_w = na_heads * HEAD_DIM
    wa_w = wa_heads * HEAD_DIM
    kv_w = WA_KV_HEADS * HEAD_DIM
    p = _proj([hl], w_in, e, BF16, name="even_w_in")
    pc = _proj([hc], w_in, e, BF16, name="even_w_in_ctx")
    oa = _neighborhood_attn(p, pc, rpb[e], na_heads)
    ob = _window_gqa(p, pc, sink[e].astype(F32), 3 * na_w, 3 * na_w + wa_w, 3 * na_w + wa_w + kv_w,
                     wa_heads, rope)
    yl = _proj([oa, ob], w_out, e, BF16, name="even_w_out")
    yc = None
    if ctx_live:
        yc = _proj([_ctx_attn(pc, sink[e], na_heads, wa_heads)], w_out, e, BF16, name="even_w_out_ctx")
    return yl, yc


def _odd_mixer(h, w_in, conv_w, w_out, o):
    sc_w = conv_w.shape[2]
    f_w = w_in.shape[2] - 3 * sc_w
    p = _proj([h], w_in, o, BF16, name="odd_w_in")
    f = _seq_dft(_group_dft(p, f_w))
    sc = _shortconv(p, conv_w, o, f_w, sc_w)
    return _proj([f, sc], w_out, o, BF16, name="odd_w_out")


def _conv_ffn(h, w_up, conv_w, w_down, i):
    u = _proj([h], w_up, i, BF16, name="ffn_w_up")
    a = _ffn_gate(u, conv_w, i)
    return _proj([a], w_down, i, BF16, tm=512, tn=1024, name="ffn_w_down")


def kernel(x, c, ctx, c_ctx, w_mod, b_mod, g_mix_pre, g_mix_post, g_ffn_pre, g_ffn_post, even_w_in, even_rpb, even_sink, even_w_out, odd_w_in, odd_conv, odd_w_out, ffn_w_up, ffn_conv, ffn_w_down):
    b, t, d = x.shape
    depth = w_mod.shape[0]
    rows = -(-(b + 1) // 8) * 8
    cond = jnp.zeros((rows, d), F32).at[:b].set(c).at[b].set(c_ctx)
    mod = _modulation(cond, w_mod, b_mod)
    rope = _rope_tables(t)
    xc = ctx
    hl = _prenorm(x, g_mix_pre[0], mod, 0, 0)
    for i in range(depth):
        ctx_live = any(j % 2 == 0 for j in range(i + 1, depth))
        need_hc = (i % 2 == 0) or ctx_live
        hc = _prenorm(xc, g_mix_pre[i], mod, i, 0, fixed_row=b) if need_hc else None
        if i % 2 == 0:
            e = i // 2
            yl, yc = _even_mixer(hl, hc, even_w_in, even_rpb, even_sink, even_w_out, e, rope, ctx_live)
        else:
            o = i // 2
            yl = _odd_mixer(hl, odd_w_in, odd_conv, odd_w_out, o)
            yc = _odd_mixer(hc, odd_w_in, odd_conv, odd_w_out, o) if ctx_live else None
        x, hl = _residual(x, yl, g_mix_post[i], mod, i, 2, nxt=(g_ffn_pre[i], i, 3))
        yf = _conv_ffn(hl, ffn_w_up, ffn_conv, ffn_w_down, i)
        nxt = (g_mix_pre[i + 1], i + 1, 0) if i + 1 < depth else None
        x, hl = _residual(x, yf, g_ffn_post[i], mod, i, 5, nxt=nxt)
        if ctx_live:
            xc, hcf = _residual(xc, yc, g_mix_post[i], mod, i, 2, nxt=(g_ffn_pre[i], i, 3), fixed_row=b)
            yfc = _conv_ffn(hcf, ffn_w_up, ffn_conv, ffn_w_down, i)
            xc, _ = _residual(xc, yfc, g_ffn_post[i], mod, i, 5, fixed_row=b)
    return x
```

```python
import functools
import math

import numpy as np
import jax
import jax.numpy as jnp
from jax import lax
from jax.experimental import pallas as pl
from jax.experimental.pallas import tpu as pltpu

F32 = jnp.float32
BF16 = jnp.bfloat16

GRID_W = 64
HEAD_DIM = 128
NA_KH = 8
NA_KW = 16
WA_KV_HEADS = 4
WA_WINDOW = 128
WA_BLOCK = 128
FNET_GROUP_DIM = 128
ROPE_THETA = 10000.0
RMS_EPS = 1e-6
NEG_INF = -1e30

V7X_VMEM_BYTES = 64 * 1024 * 1024
VMEM_LIMIT = V7X_VMEM_BYTES - 8 * 1024 * 1024
LANES = 128

NA_ROW_BLOCK = 4
NA_HEADS_PER_STEP = 8
WA_KV_PER_STEP = 4
CTX_HEADS_PER_STEP = 4
CONV_BLOCK_ELEMS = 4096 * 128
F32_SUBLANES = 8
MIN_ROW_TILES_FOR_STAGING = 8


def _params(n_grid):
    return pltpu.CompilerParams(dimension_semantics=("arbitrary",) * n_grid,
                                vmem_limit_bytes=VMEM_LIMIT)


def _pick(n, pref):
    if n <= pref:
        return n
    t = pref
    while n % t:
        t //= 2
    return t


def _mod_kernel(c_ref, w_ref, b_ref, o_ref):
    c = c_ref[...]
    s = (c / (1.0 + jnp.exp(-c))).astype(BF16)
    y = jnp.dot(s, w_ref[...].astype(BF16), preferred_element_type=F32)
    o_ref[...] = y + b_ref[...]


def _modulation(cond, w_mod, b_mod):
    depth, d, n = w_mod.shape
    rows = cond.shape[0]
    tn = _pick(n, 512)
    return pl.pallas_call(
        _mod_kernel,
        grid=(depth, n // tn),
        in_specs=[pl.BlockSpec((rows, d), lambda l, j: (0, 0)),
                  pl.BlockSpec((None, d, tn), lambda l, j: (l, 0, j)),
                  pl.BlockSpec((None, 1, tn), lambda l, j: (l, 0, j))],
        out_specs=pl.BlockSpec((None, rows, tn), lambda l, j: (l, 0, j)),
        out_shape=jax.ShapeDtypeStruct((depth, rows, n), F32),
        compiler_params=_params(2),
        name="adaln_mod",
    )(cond, w_mod, b_mod.reshape(depth, 1, n))


def _rms(v, g):
    return v * lax.rsqrt(jnp.mean(v * v, axis=-1, keepdims=True) + RMS_EPS) * g


def _mod_row(ref, fixed_row):
    row = pl.program_id(0) if fixed_row is None else fixed_row
    return ref[pl.ds(row, 1), :]


def _prenorm_kernel(x_ref, g_ref, sh_ref, sc_ref, h_ref, *, fixed_row):
    sh = _mod_row(sh_ref, fixed_row)
    sc = _mod_row(sc_ref, fixed_row)
    h_ref[...] = (_rms(x_ref[...], g_ref[...]) * (1.0 + sc) + sh).astype(h_ref.dtype)


def _prenorm(x, g, mod, layer, slot, fixed_row=None):
    b, t, d = x.shape
    rows = mod.shape[1]
    tr = _pick(t, 256)
    return pl.pallas_call(
        functools.partial(_prenorm_kernel, fixed_row=fixed_row),
        grid=(b, t // tr),
        in_specs=[pl.BlockSpec((None, tr, d), lambda i, j: (i, j, 0)),
                  pl.BlockSpec((1, d), lambda i, j: (0, 0)),
                  pl.BlockSpec((None, rows, d), lambda i, j: (layer, 0, slot)),
                  pl.BlockSpec((None, rows, d), lambda i, j: (layer, 0, slot + 1))],
        out_specs=pl.BlockSpec((None, tr, d), lambda i, j: (i, j, 0)),
        out_shape=jax.ShapeDtypeStruct((b, t, d), BF16),
        compiler_params=_params(2),
        name="prenorm",
    )(x, g.reshape(1, d), mod, mod)


def _resid_kernel(x_ref, y_ref, gp_ref, gt_ref, *rest, fixed_row, with_next):
    gt = _mod_row(gt_ref, fixed_row)
    xn = x_ref[...] + gt * _rms(y_ref[...].astype(F32), gp_ref[...])
    if with_next:
        gn_ref, sh_ref, sc_ref, xo_ref, h_ref = rest
        sh = _mod_row(sh_ref, fixed_row)
        sc = _mod_row(sc_ref, fixed_row)
        h_ref[...] = (_rms(xn, gn_ref[...]) * (1.0 + sc) + sh).astype(h_ref.dtype)
    else:
        (xo_ref,) = rest
    xo_ref[...] = xn


def _residual(x, y, g_post, mod, layer, gate_slot, nxt=None, fixed_row=None):
    b, t, d = x.shape
    rows = mod.shape[1]
    tr = _pick(t, 256)
    row_spec = pl.BlockSpec((None, tr, d), lambda i, j: (i, j, 0))
    vec_spec = pl.BlockSpec((1, d), lambda i, j: (0, 0))

    def mod_spec(l, s):
        return pl.BlockSpec((None, rows, d), lambda i, j: (l, 0, s))

    in_specs = [row_spec, row_spec, vec_spec, mod_spec(layer, gate_slot)]
    args = [x, y, g_post.reshape(1, d), mod]
    out_specs = [row_spec]
    out_shape = [jax.ShapeDtypeStruct((b, t, d), F32)]
    if nxt is not None:
        g_pre, nl, ns = nxt
        in_specs += [vec_spec, mod_spec(nl, ns), mod_spec(nl, ns + 1)]
        args += [g_pre.reshape(1, d), mod, mod]
        out_specs.append(row_spec)
        out_shape.append(jax.ShapeDtypeStruct((b, t, d), BF16))
    out = pl.pallas_call(
        functools.partial(_resid_kernel, fixed_row=fixed_row, with_next=nxt is not None),
        grid=(b, t // tr),
        in_specs=in_specs, out_specs=out_specs, out_shape=out_shape,
        compiler_params=_params(2),
        name="residual_norm",
    )(*args)
    return (out[0], out[1]) if nxt is not None else (out[0], None)


def _dot_parts(a_refs, w_ref, o_ref):
    acc = None
    k0 = 0
    for a_ref in a_refs:
        kp = a_ref.shape[1]
        part = jnp.dot(a_ref[...], w_ref[k0:k0 + kp, :], preferred_element_type=F32)
        acc = part if acc is None else acc + part
        k0 += kp
    o_ref[...] = acc.astype(o_ref.dtype)


def _mm_kernel(*refs, n_parts):
    _dot_parts(refs[:n_parts], refs[n_parts], refs[n_parts + 1])


def _mm_cast_kernel(*refs, n_parts):
    a_refs = refs[:n_parts]
    w_ref, o_ref, wbf = refs[n_parts:]

    @pl.when(pl.program_id(1) == 0)
    def _():
        wbf[...] = w_ref[...].astype(BF16)

    _dot_parts(a_refs, wbf, o_ref)


def _mm_staged_kernel(*refs, n_parts, layer, tn):
    a_refs = refs[:n_parts]
    w_hbm, o_ref, stage, wbf, sem = refs[n_parts:]
    j = pl.program_id(0)

    def tile_copy(jj):
        return pltpu.make_async_copy(w_hbm.at[layer, :, pl.ds(pl.multiple_of(jj * tn, tn), tn)], stage, sem)

    @pl.when(pl.program_id(1) == 0)
    def _():
        @pl.when(j == 0)
        def _():
            tile_copy(0).start()

        tile_copy(j).wait()
        wbf[...] = stage[...].astype(BF16)

        @pl.when(j + 1 < pl.num_programs(0))
        def _():
            tile_copy(j + 1).start()

    _dot_parts(a_refs, wbf, o_ref)


def _matmul(a_parts, w, layer, out_dtype, *, tm=1024, tn=1024, name="matmul"):
    m = a_parts[0].shape[0]
    _, k, n = w.shape
    assert sum(a.shape[1] for a in a_parts) == k
    tm = _pick(m, tm)
    n_parts = len(a_parts)
    a_specs = [pl.BlockSpec((tm, a.shape[1]), lambda j, i: (i, 0)) for a in a_parts]
    out_shape = jax.ShapeDtypeStruct((m, n), out_dtype)
    if m // tm >= MIN_ROW_TILES_FOR_STAGING:
        tn = _pick(n, tn)
        return pl.pallas_call(
            functools.partial(_mm_staged_kernel, n_parts=n_parts, layer=layer, tn=tn),
            grid=(n // tn, m // tm),
            in_specs=a_specs + [pl.BlockSpec(memory_space=pl.ANY)],
            out_specs=pl.BlockSpec((tm, tn), lambda j, i: (i, j)),
            out_shape=out_shape,
            scratch_shapes=[pltpu.VMEM((k, tn), F32), pltpu.VMEM((k, tn), BF16), pltpu.SemaphoreType.DMA(())],
            compiler_params=_params(2),
            name=name,
        )(*a_parts, w)
    tn = _pick(n, tn // 2)
    return pl.pallas_call(
        functools.partial(_mm_cast_kernel, n_parts=n_parts),
        grid=(n // tn, m // tm),
        in_specs=a_specs + [pl.BlockSpec((None, k, tn), lambda j, i: (layer, 0, j))],
        out_specs=pl.BlockSpec((tm, tn), lambda j, i: (i, j)),
        out_shape=out_shape,
        scratch_shapes=[pltpu.VMEM((k, tn), BF16)],
        compiler_params=_params(2),
        name=name,
    )(*a_parts, w)


def _na_geometry(rows, rb):
    kh = min(NA_KH, rows)
    kr = min(rb + kh, rows)
    nblk = rows // rb
    slab = np.clip(np.arange(nblk) * rb - kh // 2, 0, rows - kr)
    variants, variant_of = [], []
    for j in range(nblk):
        geo = []
        for a in range(rb):
            r = j * rb + a
            rs = int(np.clip(r - kh // 2, 0, rows - kh))
            lo = rs - int(slab[j])
            geo.append((lo, lo + kh, rs - r + NA_KH - 1))
        geo = tuple(geo)
        if geo not in variants:
            variants.append(geo)
        variant_of.append(variants.index(geo))
    return kr, slab, np.asarray(variant_of), variants


def _na_bias(rpb, rows, rb):
    kr, slab, variant_of, variants = _na_geometry(rows, rb)
    h = rpb.shape[0]
    qcol = np.arange(GRID_W)
    cstart = np.clip(qcol - NA_KW // 2, 0, GRID_W - NA_KW)
    kcol = np.arange(GRID_W)
    col_ok = (kcol[None, :] >= cstart[:, None]) & (kcol[None, :] < cstart[:, None] + NA_KW)
    col_rel = np.clip(kcol[None, :] - qcol[:, None], -(NA_KW - 1), NA_KW - 1) + NA_KW - 1
    e = jnp.take(rpb, jnp.asarray(col_rel.reshape(-1)), axis=2).reshape(h, 2 * NA_KH - 1, GRID_W, GRID_W)
    e = jnp.where(jnp.asarray(col_ok)[None, None], e, NEG_INF)
    e = e.transpose(0, 2, 1, 3).reshape(h, GRID_W, (2 * NA_KH - 1) * GRID_W)
    out = []
    for geo in variants:
        rows_v = []
        for lo, hi, rr0 in geo:
            seg = e[:, :, rr0 * GRID_W:(rr0 + hi - lo) * GRID_W]
            rows_v.append(jnp.pad(seg, ((0, 0), (0, 0), (lo * GRID_W, (kr - hi) * GRID_W)),
                                  constant_values=NEG_INF))
        out.append(jnp.concatenate(rows_v, axis=1))
    return jnp.stack(out, axis=1).astype(F32), kr, slab, variant_of


def _nt_dot(a, b):
    return lax.dot_general(a, b, (((1,), (1,)), ((), ())), preferred_element_type=F32)


def _na_kernel(slab_ref, var_ref, q_ref, k_ref, v_ref, kc_ref, vc_ref, bias_ref, o_ref, *, kr, hp):
    del var_ref
    j = pl.program_id(2)
    ks = pl.multiple_of(slab_ref[j] * GRID_W, GRID_W)
    nk = kr * GRID_W
    hd = HEAD_DIM
    scale = hd ** -0.5
    for hh in range(hp):
        cols = slice(hh * hd, (hh + 1) * hd)
        q = q_ref[:, cols]
        s_loc = _nt_dot(q, k_ref[pl.ds(ks, nk), cols]) * scale + bias_ref[hh]
        s_ctx = _nt_dot(q, kc_ref[:, cols]) * scale
        m = jnp.maximum(jnp.max(s_loc, axis=-1, keepdims=True), jnp.max(s_ctx, axis=-1, keepdims=True))
        p_loc = jnp.exp(s_loc - m)
        p_ctx = jnp.exp(s_ctx - m)
        v_ext = jnp.concatenate([v_ref[pl.ds(ks, nk), cols], jnp.ones((nk, hd), BF16)], axis=1)
        vc_ext = jnp.concatenate([vc_ref[:, cols], jnp.ones((vc_ref.shape[0], hd), BF16)], axis=1)
        oe = (jnp.dot(p_loc.astype(BF16), v_ext, preferred_element_type=F32)
              + jnp.dot(p_ctx.astype(BF16), vc_ext, preferred_element_type=F32))
        o_ref[:, cols] = (oe[:, :hd] / oe[:, hd:hd + 1]).astype(o_ref.dtype)


def _neighborhood_attn(p, pc, rpb, n_heads):
    b, t, _ = p.shape
    l = pc.shape[1]
    rows = t // GRID_W
    rb = min(NA_ROW_BLOCK, rows)
    hp = min(NA_HEADS_PER_STEP, n_heads)
    assert n_heads % hp == 0
    ng = n_heads // hp
    bias, kr, slab, variant_of = _na_bias(rpb, rows, rb)
    nblk = rows // rb
    tq = rb * GRID_W
    w = hp * HEAD_DIM
    grid_spec = pltpu.PrefetchScalarGridSpec(
        num_scalar_prefetch=2,
        grid=(b, ng, nblk),
        in_specs=[
            pl.BlockSpec((None, tq, w), lambda i, h, j, s, v: (i, j, h)),
            pl.BlockSpec((None, t, w), lambda i, h, j, s, v: (i, 0, ng + h)),
            pl.BlockSpec((None, t, w), lambda i, h, j, s, v: (i, 0, 2 * ng + h)),
            pl.BlockSpec((None, l, w), lambda i, h, j, s, v: (i, 0, ng + h)),
            pl.BlockSpec((None, l, w), lambda i, h, j, s, v: (i, 0, 2 * ng + h)),
            pl.BlockSpec((hp, None, tq, kr * GRID_W), lambda i, h, j, s, v: (h, v[j], 0, 0)),
        ],
        out_specs=pl.BlockSpec((None, tq, w), lambda i, h, j, s, v: (i, j, h)),
    )
    return pl.pallas_call(
        functools.partial(_na_kernel, kr=kr, hp=hp),
        grid_spec=grid_spec,
        out_shape=jax.ShapeDtypeStruct((b, t, n_heads * HEAD_DIM), BF16),
        compiler_params=_params(3),
        name="neighborhood_attn",
    )(jnp.asarray(slab, jnp.int32), jnp.asarray(variant_of, jnp.int32), p, p, p, pc, pc, bias)


def _rope_tables(s):
    tpos = jnp.arange(s)
    row = (tpos // GRID_W).astype(F32)
    col = (tpos % GRID_W).astype(F32)
    nf = HEAD_DIM // 4
    inv = ROPE_THETA ** (-jnp.arange(nf, dtype=F32) / nf)
    lane = np.arange(HEAD_DIM)
    axis = lane // (2 * nf)
    second = (lane % (2 * nf)) >= nf
    freq = lane % nf
    pos = jnp.where(jnp.asarray(axis)[None, :] == 0, row[:, None], col[:, None])
    ang = pos * inv[jnp.asarray(freq)][None, :]
    cos, sin = jnp.cos(ang), jnp.sin(ang)
    sec = jnp.asarray(second)[None, :]
    return cos, jnp.where(sec, 0.0, -sin), jnp.where(sec, sin, 0.0)


def _rope(x, c, s1, s2):
    nf = HEAD_DIM // 4
    return x * c + pltpu.roll(x, HEAD_DIM - nf, axis=1) * s1 + pltpu.roll(x, nf, axis=1) * s2


def _wa_kernel(sink_ref, q_ref, k_ref, v_ref, kc_ref, vc_ref, c_ref, s1_ref, s2_ref, o_ref,
               *, group, kp, seq):
    kg = pl.program_id(1)
    n = pl.program_id(2)
    wb, hd = WA_BLOCK, HEAD_DIM
    nl = 3 * wb
    q0 = pl.multiple_of(n * wb, wb)
    bs = pl.multiple_of(jnp.clip(n * wb - wb, 0, seq - nl), wb)
    scale = hd ** -0.5
    cq, s1q, s2q = c_ref[pl.ds(q0, wb), :], s1_ref[pl.ds(q0, wb), :], s2_ref[pl.ds(q0, wb), :]
    ck, s1k, s2k = c_ref[pl.ds(bs, nl), :], s1_ref[pl.ds(bs, nl), :], s2_ref[pl.ds(bs, nl), :]
    rowi = lax.broadcasted_iota(jnp.int32, (group * wb, nl), 0)
    kpos = bs + lax.broadcasted_iota(jnp.int32, (group * wb, nl), 1)
    qpos = q0 + (rowi & (wb - 1))
    in_window = jnp.abs(kpos - qpos) <= WA_WINDOW
    gi = lax.broadcasted_iota(jnp.int32, (group * wb, 1), 0) // wb
    for kk in range(kp):
        kcols = slice(kk * hd, (kk + 1) * hd)
        q = jnp.concatenate(
            [_rope(q_ref[:, (kk * group + g) * hd:(kk * group + g + 1) * hd].astype(F32), cq, s1q, s2q)
             .astype(BF16) for g in range(group)], axis=0)
        kb = _rope(k_ref[pl.ds(bs, nl), kcols].astype(F32), ck, s1k, s2k).astype(BF16)
        s_loc = jnp.where(in_window, _nt_dot(q, kb) * scale, NEG_INF)
        s_ctx = _nt_dot(q, kc_ref[:, kcols]) * scale
        sink = jnp.zeros((group * wb, 1), F32)
        for g in range(group):
            sink = jnp.where(gi == g, sink_ref[(kg * kp + kk) * group + g], sink)
        m = jnp.maximum(jnp.maximum(jnp.max(s_loc, axis=-1, keepdims=True),
                                    jnp.max(s_ctx, axis=-1, keepdims=True)), sink)
        p_loc = jnp.exp(s_loc - m)
        p_ctx = jnp.exp(s_ctx - m)
        den = (jnp.sum(p_loc, axis=-1, keepdims=True) + jnp.sum(p_ctx, axis=-1, keepdims=True)
               + jnp.exp(sink - m))
        o = (jnp.dot(p_loc.astype(BF16), v_ref[pl.ds(bs, nl), kcols], preferred_element_type=F32)
             + jnp.dot(p_ctx.astype(BF16), vc_ref[:, kcols], preferred_element_type=F32))
        o = (o / den).astype(o_ref.dtype)
        for g in range(group):
            o_ref[:, (kk * group + g) * hd:(kk * group + g + 1) * hd] = o[g * wb:(g + 1) * wb, :]


def _window_gqa(p, pc, sink, q_col, k_col, v_col, n_heads, rope):
    b, t, _ = p.shape
    l = pc.shape[1]
    kvh = WA_KV_HEADS
    group = n_heads // kvh
    kp = min(WA_KV_PER_STEP, kvh)
    hd, wb = HEAD_DIM, WA_BLOCK
    qw = kp * group * hd
    kw = kp * hd
    assert kvh % kp == 0 and q_col % qw == 0 and k_col % kw == 0 and v_col % kw == 0
    assert t % wb == 0 and t >= 3 * wb
    cos, s1, s2 = rope
    tab_spec = pl.BlockSpec((t, hd), lambda i, k, n: (0, 0))
    return pl.pallas_call(
        functools.partial(_wa_kernel, group=group, kp=kp, seq=t),
        grid=(b, kvh // kp, t // wb),
        in_specs=[
            pl.BlockSpec(memory_space=pltpu.SMEM),
            pl.BlockSpec((None, wb, qw), lambda i, k, n: (i, n, q_col // qw + k)),
            pl.BlockSpec((None, t, kw), lambda i, k, n: (i, 0, k_col // kw + k)),
            pl.BlockSpec((None, t, kw), lambda i, k, n: (i, 0, v_col // kw + k)),
            pl.BlockSpec((None, l, kw), lambda i, k, n: (i, 0, k_col // kw + k)),
            pl.BlockSpec((None, l, kw), lambda i, k, n: (i, 0, v_col // kw + k)),
            tab_spec, tab_spec, tab_spec,
        ],
        out_specs=pl.BlockSpec((None, wb, qw), lambda i, k, n: (i, n, k)),
        out_shape=jax.ShapeDtypeStruct((b, t, n_heads * hd), BF16),
        compiler_params=_params(3),
        name="window_gqa",
    )(sink, p, p, p, pc, pc, cos, s1, s2)


def _ctx_kernel(sink_ref, q_ref, *refs, hp):
    k_refs, v_refs, o_ref = refs[:hp], refs[hp:2 * hp], refs[2 * hp]
    hd = HEAD_DIM
    scale = hd ** -0.5
    for hh in range(hp):
        cols = slice(hh * hd, (hh + 1) * hd)
        s = _nt_dot(q_ref[:, cols], k_refs[hh][...]) * scale
        sink = sink_ref[pl.program_id(1) * hp + hh]
        m = jnp.maximum(jnp.max(s, axis=-1, keepdims=True), sink)
        p = jnp.exp(s - m)
        den = jnp.sum(p, axis=-1, keepdims=True) + jnp.exp(sink - m)
        o = jnp.dot(p.astype(BF16), v_refs[hh][...], preferred_element_type=F32)
        o_ref[:, cols] = (o / den).astype(o_ref.dtype)


def _ctx_attn(pc, sink, na_heads, wa_heads):
    b, l, _ = pc.shape
    hd = HEAD_DIM
    group = wa_heads // WA_KV_HEADS
    hp = CTX_HEADS_PER_STEP
    while na_heads % hp or wa_heads % hp:
        hp //= 2
    qb0 = 3 * na_heads
    kb0 = qb0 + wa_heads
    vb0 = kb0 + WA_KV_HEADS
    sink_all = jnp.concatenate([jnp.full((na_heads,), NEG_INF, F32), sink.astype(F32)])

    def qmap(i, g):
        return (i, 0, jnp.where(g * hp < na_heads, g, (qb0 - na_heads) // hp + g))

    def kv_spec(hh, mha0, gqa0):
        def index(i, g):
            h = g * hp + hh
            return (i, 0, jnp.where(h < na_heads, mha0 + h, gqa0 + (h - na_heads) // group))
        return pl.BlockSpec((None, l, hd), index)

    return pl.pallas_call(
        functools.partial(_ctx_kernel, hp=hp),
        grid=(b, (na_heads + wa_heads) // hp),
        in_specs=[pl.BlockSpec(memory_space=pltpu.SMEM), pl.BlockSpec((None, l, hp * hd), qmap)]
        + [kv_spec(hh, na_heads, kb0) for hh in range(hp)]
        + [kv_spec(hh, 2 * na_heads, vb0) for hh in range(hp)],
        out_specs=pl.BlockSpec((None, l, hp * hd), lambda i, g: (i, 0, g)),
        out_shape=jax.ShapeDtypeStruct((b, l, (na_heads + wa_heads) * hd), BF16),
        compiler_params=_params(2),
        name="context_attn",
    )(sink_all, pc, *([pc] * (2 * hp)))


def _dft_cos_sin(n, ks, nt):
    lo = 1
    while lo * lo < nt:
        lo *= 2
    hi = nt // lo
    k = jnp.asarray(ks, dtype=jnp.int32)
    a = (2.0 * np.pi / n) * ((k[:, None] * (jnp.arange(hi, dtype=jnp.int32) * lo)[None, :]) % n).astype(F32)
    bb = (2.0 * np.pi / n) * ((k[:, None] * jnp.arange(lo, dtype=jnp.int32)[None, :]) % n).astype(F32)
    ca, sa, cb, sb = jnp.cos(a)[:, :, None], jnp.sin(a)[:, :, None], jnp.cos(bb)[:, None, :], jnp.sin(bb)[:, None, :]
    return (ca * cb - sa * sb).reshape(len(ks), nt), (sa * cb + ca * sb).reshape(len(ks), nt)


def _group_dft_kernel(ua_ref, ub_ref, cs_ref, o_ref, *, groups):
    gd = FNET_GROUP_DIM
    cs = cs_ref[...]
    for g in range(groups):
        cols = slice(g * gd, (g + 1) * gd)
        ra = jnp.dot(ua_ref[:, cols], cs, preferred_element_type=F32)
        rb = jnp.dot(ub_ref[:, cols], cs, preferred_element_type=F32)
        plus = (ra + rb).astype(o_ref.dtype)
        minus = (ra - rb).astype(o_ref.dtype)
        o_ref[0, 0, :, cols] = plus[:, :gd]
        o_ref[0, 1, :, cols] = plus[:, gd:]
        o_ref[1, 0, :, cols] = minus[:, :gd]
        o_ref[1, 1, :, cols] = minus[:, gd:]


def _group_dft(p, width):
    b, t, _ = p.shape
    gd = FNET_GROUP_DIM
    c, s = _dft_cos_sin(gd, np.arange(gd), gd)
    cs = (jnp.concatenate([c, s], axis=1) * (gd ** -0.5)).astype(BF16)
    th = t // 2
    tt = _pick(th, 512)
    nh = th // tt
    return pl.pallas_call(
        functools.partial(_group_dft_kernel, groups=width // gd),
        grid=(b, nh),
        in_specs=[pl.BlockSpec((None, tt, width), lambda i, j: (i, j, 0)),
                  pl.BlockSpec((None, tt, width), lambda i, j: (i, nh + j, 0)),
                  pl.BlockSpec((gd, 2 * gd), lambda i, j: (0, 0))],
        out_specs=pl.BlockSpec((None, 2, 2, tt, width), lambda i, j: (i, 0, 0, j, 0)),
        out_shape=jax.ShapeDtypeStruct((b, 2, 2, th, width), BF16),
        compiler_params=_params(2),
        name="group_dft",
    )(p, p, cs)


def _seq_dft(v):
    b, _, _, th, w = v.shape
    t = 2 * th
    mats = []
    for parity in range(2):
        c, s = _dft_cos_sin(t, np.arange(parity, t, 2), th)
        mats.append(jnp.concatenate([c, -s], axis=1) * (t ** -0.5))
    a = jnp.stack(mats).astype(BF16)
    tm = _pick(th, 1024)
    tn = _pick(w, 1024)
    nj = w // tn
    out = pl.pallas_call(
        functools.partial(_mm_kernel, n_parts=1),
        grid=(b, 2, nj, th // tm),
        in_specs=[pl.BlockSpec((None, tm, t), lambda g, q, j, i: (q, i, 0)),
                  pl.BlockSpec((None, None, t, tn), lambda g, q, j, i: (g, q, 0, j))],
        out_specs=pl.BlockSpec((None, tm, tn), lambda g, q, j, i: (g, i, q * nj + j)),
        out_shape=jax.ShapeDtypeStruct((b, th, 2 * w), BF16),
        compiler_params=_params(4),
        name="seq_dft",
    )(a, v.reshape(b, 2, t, w))
    return out.reshape(b, t, w)


def _conv_cols(t, width):
    tc = LANES
    while tc * 2 * t <= CONV_BLOCK_ELEMS and width % (tc * 2) == 0:
        tc *= 2
    return tc


def _dwconv3(z, w_ref):
    t = z.shape[0]
    sub = min(F32_SUBLANES, t)
    row = lax.broadcasted_iota(jnp.int32, (sub, z.shape[1]), 0)
    zp = pltpu.roll(z, 1, axis=0)
    zn = pltpu.roll(z, t - 1, axis=0)
    zp = jnp.concatenate([jnp.where(row == 0, 0.0, zp[:sub]), zp[sub:]], axis=0)
    zn = jnp.concatenate([zn[:t - sub], jnp.where(row == sub - 1, 0.0, zn[t - sub:])], axis=0)
    return zp * w_ref[0:1, :] + z * w_ref[1:2, :] + zn * w_ref[2:3, :]


def _shortconv_kernel(bg_ref, cg_ref, hv_ref, w_ref, o_ref):
    z = cg_ref[...].astype(F32) * hv_ref[...].astype(F32)
    o_ref[...] = (bg_ref[...].astype(F32) * _dwconv3(z, w_ref)).astype(o_ref.dtype)


def _shortconv(p, conv_w, layer, col0, width):
    b, t, _ = p.shape
    tc = _conv_cols(t, math.gcd(width, col0))
    nb = width // tc
    c0 = col0 // tc
    return pl.pallas_call(
        _shortconv_kernel,
        grid=(b, nb),
        in_specs=[pl.BlockSpec((None, t, tc), lambda i, j: (i, 0, c0 + j)),
                  pl.BlockSpec((None, t, tc), lambda i, j: (i, 0, c0 + nb + j)),
                  pl.BlockSpec((None, t, tc), lambda i, j: (i, 0, c0 + 2 * nb + j)),
                  pl.BlockSpec((None, 3, tc), lambda i, j: (layer, 0, j))],
        out_specs=pl.BlockSpec((None, t, tc), lambda i, j: (i, 0, j)),
        out_shape=jax.ShapeDtypeStruct((b, t, width), BF16),
        compiler_params=_params(2),
        name="shortconv_gate",
    )(p, p, p, conv_w)


def _ffn_gate_kernel(g_ref, v_ref, wg_ref, wv_ref, o_ref):
    gate = _dwconv3(g_ref[...].astype(F32), wg_ref)
    val = _dwconv3(v_ref[...].astype(F32), wv_ref)
    o_ref[...] = (gate / (1.0 + jnp.exp(-gate)) * val).astype(o_ref.dtype)


def _ffn_gate(u, conv_w, layer):
    b, t, f2 = u.shape
    f = f2 // 2
    tc = _conv_cols(t, f)
    nb = f // tc
    return pl.pallas_call(
        _ffn_gate_kernel,
        grid=(b, nb),
        in_specs=[pl.BlockSpec((None, t, tc), lambda i, j: (i, 0, j)),
                  pl.BlockSpec((None, t, tc), lambda i, j: (i, 0, nb + j)),
                  pl.BlockSpec((None, 3, tc), lambda i, j: (layer, 0, j)),
                  pl.BlockSpec((None, 3, tc), lambda i, j: (layer, 0, nb + j))],
        out_specs=pl.BlockSpec((None, t, tc), lambda i, j: (i, 0, j)),
        out_shape=jax.ShapeDtypeStruct((b, t, f), BF16),
        compiler_params=_params(2),
        name="ffn_conv_gate",
    )(u, u, conv_w, conv_w)


def _proj(parts, w, layer, out_dtype, **kw):
    b, t, _ = parts[0].shape
    y = _matmul([a.reshape(b * t, a.shape[2]) for a in parts], w, layer, out_dtype, **kw)
    return y.reshape(b, t, -1)


def _even_mixer(hl, hc, w_in, rpb, sink, w_out, e, rope, ctx_live):
    na_heads = rpb.shape[1]
    wa_heads = sink.shape[1]
    na_w = na_heads * HEAD_DIM
    wa_w = wa_heads * HEAD_DIM
    kv_w = WA_KV_HEADS * HEAD_DIM
    p = _proj([hl], w_in, e, BF16, name="even_w_in")
    pc = _proj([hc], w_in, e, BF16, name="even_w_in_ctx")
    oa = _neighborhood_attn(p, pc, rpb[e], na_heads)
    ob = _window_gqa(p, pc, sink[e].astype(F32), 3 * na_w, 3 * na_w + wa_w, 3 * na_w + wa_w + kv_w,
                     wa_heads, rope)
    yl = _proj([oa, ob], w_out, e, BF16, name="even_w_out")
    yc = None
    if ctx_live:
        yc = _proj([_ctx_attn(pc, sink[e], na_heads, wa_heads)], w_out, e, BF16, name="even_w_out_ctx")
    return yl, yc


def _odd_mixer(h, w_in, conv_w, w_out, o):
    sc_w = conv_w.shape[2]
    f_w = w_in.shape[2] - 3 * sc_w
    p = _proj([h], w_in, o, BF16, name="odd_w_in")
    f = _seq_dft(_group_dft(p, f_w))
    sc = _shortconv(p, conv_w, o, f_w, sc_w)
    return _proj([f, sc], w_out, o, BF16, name="odd_w_out")


def _conv_ffn(h, w_up, conv_w, w_down, i):
    u = _proj([h], w_up, i, BF16, name="ffn_w_up")
    a = _ffn_gate(u, conv_w, i)
    return _proj([a], w_down, i, BF16, tm=512, tn=1024, name="ffn_w_down")


def kernel(x, c, ctx, c_ctx, w_mod, b_mod, g_mix_pre, g_mix_post, g_ffn_pre, g_ffn_post, even_w_in, even_rpb, even_sink, even_w_out, odd_w_in, odd_conv, odd_w_out, ffn_w_up, ffn_conv, ffn_w_down):
    b, t, d = x.shape
    depth = w_mod.shape[0]
    rows = -(-(b + 1) // 8) * 8
    cond = jnp.zeros((rows, d), F32).at[:b].set(c).at[b].set(c_ctx)
    mod = _modulation(cond, w_mod, b_mod)
    rope = _rope_tables(t)
    xc = ctx
    hl = _prenorm(x, g_mix_pre[0], mod, 0, 0)
    for i in range(depth):
        ctx_live = any(j % 2 == 0 for j in range(i + 1, depth))
        need_hc = (i % 2 == 0) or ctx_live
        hc = _prenorm(xc, g_mix_pre[i], mod, i, 0, fixed_row=b) if need_hc else None
        if i % 2 == 0:
            e = i // 2
            yl, yc = _even_mixer(hl, hc, even_w_in, even_rpb, even_sink, even_w_out, e, rope, ctx_live)
        else:
            o = i // 2
            yl = _odd_mixer(hl, odd_w_in, odd_conv, odd_w_out, o)
            yc = _odd_mixer(hc, odd_w_in, odd_conv, odd_w_out, o) if ctx_live else None
        x, hl = _residual(x, yl, g_mix_post[i], mod, i, 2, nxt=(g_ffn_pre[i], i, 3))
        yf = _conv_ffn(hl, ffn_w_up, ffn_conv, ffn_w_down, i)
        nxt = (g_mix_pre[i + 1], i + 1, 0) if i + 1 < depth else None
        x, hl = _residual(x, yf, g_ffn_post[i], mod, i, 5, nxt=nxt)
        if ctx_live:
            xc, hcf = _residual(xc, yc, g_mix_post[i], mod, i, 2, nxt=(g_ffn_pre[i], i, 3), fixed_row=b)
            yfc = _conv_ffn(hcf, ffn_w_up, ffn_conv, ffn_w_down, i)
            xc, _ = _residual(xc, yfc, g_ffn_post[i], mod, i, 5, fixed_row=b)
    return x
```

```python
import functools
import math

import numpy as np
import jax
import jax.numpy as jnp
from jax import lax
from jax.experimental import pallas as pl
from jax.experimental.pallas import tpu as pltpu

F32 = jnp.float32
BF16 = jnp.bfloat16

GRID_W = 64
HEAD_DIM = 128
NA_KH = 8
NA_KW = 16
WA_KV_HEADS = 4
WA_WINDOW = 128
WA_BLOCK = 128
FNET_GROUP_DIM = 128
ROPE_THETA = 10000.0
RMS_EPS = 1e-6
NEG_INF = -1e30

V7X_VMEM_BYTES = 64 * 1024 * 1024
VMEM_LIMIT = V7X_VMEM_BYTES - 8 * 1024 * 1024
LANES = 128

NA_ROW_BLOCK = 4
NA_HEADS_PER_STEP = 8
WA_KV_PER_STEP = 4
CTX_HEADS_PER_STEP = 4
CONV_BLOCK_ELEMS = 4096 * 128
F32_SUBLANES = 8
MIN_ROW_TILES_FOR_STAGING = 8


def _params(n_grid):
    return pltpu.CompilerParams(dimension_semantics=("arbitrary",) * n_grid,
                                vmem_limit_bytes=VMEM_LIMIT)


def _pick(n, pref):
    if n <= pref:
        return n
    t = pref
    while n % t:
        t //= 2
    return t


def _mod_kernel(c_ref, w_ref, b_ref, o_ref):
    c = c_ref[...]
    s = (c / (1.0 + jnp.exp(-c))).astype(BF16)
    y = jnp.dot(s, w_ref[...].astype(BF16), preferred_element_type=F32)
    o_ref[...] = y + b_ref[...]


def _modulation(cond, w_mod, b_mod):
    depth, d, n = w_mod.shape
    rows = cond.shape[0]
    tn = _pick(n, 512)
    return pl.pallas_call(
        _mod_kernel,
        grid=(depth, n // tn),
        in_specs=[pl.BlockSpec((rows, d), lambda l, j: (0, 0)),
                  pl.BlockSpec((None, d, tn), lambda l, j: (l, 0, j)),
                  pl.BlockSpec((None, 1, tn), lambda l, j: (l, 0, j))],
        out_specs=pl.BlockSpec((None, rows, tn), lambda l, j: (l, 0, j)),
        out_shape=jax.ShapeDtypeStruct((depth, rows, n), F32),
        compiler_params=_params(2),
        name="adaln_mod",
    )(cond, w_mod, b_mod.reshape(depth, 1, n))


def _rms(v, g):
    return v * lax.rsqrt(jnp.mean(v * v, axis=-1, keepdims=True) + RMS_EPS) * g


def _mod_row(ref, fixed_row):
    row = pl.program_id(0) if fixed_row is None else fixed_row
    return ref[pl.ds(row, 1), :]


def _prenorm_kernel(x_ref, g_ref, sh_ref, sc_ref, h_ref, *, fixed_row):
    sh = _mod_row(sh_ref, fixed_row)
    sc = _mod_row(sc_ref, fixed_row)
    h_ref[...] = (_rms(x_ref[...], g_ref[...]) * (1.0 + sc) + sh).astype(h_ref.dtype)


def _prenorm(x, g, mod, layer, slot, fixed_row=None):
    b, t, d = x.shape
    rows = mod.shape[1]
    tr = _pick(t, 256)
    return pl.pallas_call(
        functools.partial(_prenorm_kernel, fixed_row=fixed_row),
        grid=(b, t // tr),
        in_specs=[pl.BlockSpec((None, tr, d), lambda i, j: (i, j, 0)),
                  pl.BlockSpec((1, d), lambda i, j: (0, 0)),
                  pl.BlockSpec((None, rows, d), lambda i, j: (layer, 0, slot)),
                  pl.BlockSpec((None, rows, d), lambda i, j: (layer, 0, slot + 1))],
        out_specs=pl.BlockSpec((None, tr, d), lambda i, j: (i, j, 0)),
        out_shape=jax.ShapeDtypeStruct((b, t, d), BF16),
        compiler_params=_params(2),
        name="prenorm",
    )(x, g.reshape(1, d), mod, mod)


def _resid_kernel(x_ref, y_ref, gp_ref, gt_ref, *rest, fixed_row, with_next):
    gt = _mod_row(gt_ref, fixed_row)
    xn = x_ref[...] + gt * _rms(y_ref[...].astype(F32), gp_ref[...])
    if with_next:
        gn_ref, sh_ref, sc_ref, xo_ref, h_ref = rest
        sh = _mod_row(sh_ref, fixed_row)
        sc = _mod_row(sc_ref, fixed_row)
        h_ref[...] = (_rms(xn, gn_ref[...]) * (1.0 + sc) + sh).astype(h_ref.dtype)
    else:
        (xo_ref,) = rest
    xo_ref[...] = xn


def _residual(x, y, g_post, mod, layer, gate_slot, nxt=None, fixed_row=None):
    b, t, d = x.shape
    rows = mod.shape[1]
    tr = _pick(t, 256)
    row_spec = pl.BlockSpec((None, tr, d), lambda i, j: (i, j, 0))
    vec_spec = pl.BlockSpec((1, d), lambda i, j: (0, 0))

    def mod_spec(l, s):
        return pl.BlockSpec((None, rows, d), lambda i, j: (l, 0, s))

    in_specs = [row_spec, row_spec, vec_spec, mod_spec(layer, gate_slot)]
    args = [x, y, g_post.reshape(1, d), mod]
    out_specs = [row_spec]
    out_shape = [jax.ShapeDtypeStruct((b, t, d), F32)]
    if nxt is not None:
        g_pre, nl, ns = nxt
        in_specs += [vec_spec, mod_spec(nl, ns), mod_spec(nl, ns + 1)]
        args += [g_pre.reshape(1, d), mod, mod]
        out_specs.append(row_spec)
        out_shape.append(jax.ShapeDtypeStruct((b, t, d), BF16))
    out = pl.pallas_call(
        functools.partial(_resid_kernel, fixed_row=fixed_row, with_next=nxt is not None),
        grid=(b, t // tr),
        in_specs=in_specs, out_specs=out_specs, out_shape=out_shape,
        compiler_params=_params(2),
        name="residual_norm",
    )(*args)
    return (out[0], out[1]) if nxt is not None else (out[0], None)


def _dot_parts(a_refs, w_ref, o_ref):
    acc = None
    k0 = 0
    for a_ref in a_refs:
        kp = a_ref.shape[1]
        part = jnp.dot(a_ref[...], w_ref[k0:k0 + kp, :], preferred_element_type=F32)
        acc = part if acc is None else acc + part
        k0 += kp
    o_ref[...] = acc.astype(o_ref.dtype)


def _mm_kernel(*refs, n_parts):
    _dot_parts(refs[:n_parts], refs[n_parts], refs[n_parts + 1])


def _mm_cast_kernel(*refs, n_parts):
    a_refs = refs[:n_parts]
    w_ref, o_ref, wbf = refs[n_parts:]

    @pl.when(pl.program_id(1) == 0)
    def _():
        wbf[...] = w_ref[...].astype(BF16)

    _dot_parts(a_refs, wbf, o_ref)


def _mm_staged_kernel(*refs, n_parts, layer, tn):
    a_refs = refs[:n_parts]
    w_hbm, o_ref, stage, wbf, sem = refs[n_parts:]
    j = pl.program_id(0)

    def tile_copy(jj):
        return pltpu.make_async_copy(w_hbm.at[layer, :, pl.ds(pl.multiple_of(jj * tn, tn), tn)], stage, sem)

    @pl.when(pl.program_id(1) == 0)
    def _():
        @pl.when(j == 0)
        def _():
            tile_copy(0).start()

        tile_copy(j).wait()
        wbf[...] = stage[...].astype(BF16)

        @pl.when(j + 1 < pl.num_programs(0))
        def _():
            tile_copy(j + 1).start()

    _dot_parts(a_refs, wbf, o_ref)


def _matmul(a_parts, w, layer, out_dtype, *, tm=1024, tn=1024, name="matmul"):
    m = a_parts[0].shape[0]
    _, k, n = w.shape
    assert sum(a.shape[1] for a in a_parts) == k
    tm = _pick(m, tm)
    n_parts = len(a_parts)
    a_specs = [pl.BlockSpec((tm, a.shape[1]), lambda j, i: (i, 0)) for a in a_parts]
    out_shape = jax.ShapeDtypeStruct((m, n), out_dtype)
    if m // tm >= MIN_ROW_TILES_FOR_STAGING:
        tn = _pick(n, tn)
        return pl.pallas_call(
            functools.partial(_mm_staged_kernel, n_parts=n_parts, layer=layer, tn=tn),
            grid=(n // tn, m // tm),
            in_specs=a_specs + [pl.BlockSpec(memory_space=pl.ANY)],
            out_specs=pl.BlockSpec((tm, tn), lambda j, i: (i, j)),
            out_shape=out_shape,
            scratch_shapes=[pltpu.VMEM((k, tn), F32), pltpu.VMEM((k, tn), BF16), pltpu.SemaphoreType.DMA(())],
            compiler_params=_params(2),
            name=name,
        )(*a_parts, w)
    tn = _pick(n, tn // 2)
    return pl.pallas_call(
        functools.partial(_mm_cast_kernel, n_parts=n_parts),
        grid=(n // tn, m // tm),
        in_specs=a_specs + [pl.BlockSpec((None, k, tn), lambda j, i: (layer, 0, j))],
        out_specs=pl.BlockSpec((tm, tn), lambda j, i: (i, j)),
        out_shape=out_shape,
        scratch_shapes=[pltpu.VMEM((k, tn), BF16)],
        compiler_params=_params(2),
        name=name,
    )(*a_parts, w)


def _na_geometry(rows, rb):
    kh = min(NA_KH, rows)
    kr = min(rb + kh, rows)
    nblk = rows // rb
    slab = np.clip(np.arange(nblk) * rb - kh // 2, 0, rows - kr)
    variants, variant_of = [], []
    for j in range(nblk):
        geo = []
        for a in range(rb):
            r = j * rb + a
            rs = int(np.clip(r - kh // 2, 0, rows - kh))
            lo = rs - int(slab[j])
            geo.append((lo, lo + kh, rs - r + NA_KH - 1))
        geo = tuple(geo)
        if geo not in variants:
            variants.append(geo)
        variant_of.append(variants.index(geo))
    return kr, slab, np.asarray(variant_of), variants


def _na_bias(rpb, rows, rb):
    kr, slab, variant_of, variants = _na_geometry(rows, rb)
    h = rpb.shape[0]
    qcol = np.arange(GRID_W)
    cstart = np.clip(qcol - NA_KW // 2, 0, GRID_W - NA_KW)
    kcol = np.arange(GRID_W)
    col_ok = (kcol[None, :] >= cstart[:, None]) & (kcol[None, :] < cstart[:, None] + NA_KW)
    col_rel = np.clip(kcol[None, :] - qcol[:, None], -(NA_KW - 1), NA_KW - 1) + NA_KW - 1
    e = jnp.take(rpb, jnp.asarray(col_rel.reshape(-1)), axis=2).reshape(h, 2 * NA_KH - 1, GRID_W, GRID_W)
    e = jnp.where(jnp.asarray(col_ok)[None, None], e, NEG_INF)
    e = e.transpose(0, 2, 1, 3).reshape(h, GRID_W, (2 * NA_KH - 1) * GRID_W)
    out = []
    for geo in variants:
        rows_v = []
        for lo, hi, rr0 in geo:
            seg = e[:, :, rr0 * GRID_W:(rr0 + hi - lo) * GRID_W]
            rows_v.append(jnp.pad(seg, ((0, 0), (0, 0), (lo * GRID_W, (kr - hi) * GRID_W)),
                                  constant_values=NEG_INF))
        out.append(jnp.concatenate(rows_v, axis=1))
    return jnp.stack(out, axis=1).astype(F32), kr, slab, variant_of


def _nt_dot(a, b):
    return lax.dot_general(a, b, (((1,), (1,)), ((), ())), preferred_element_type=F32)


def _na_kernel(slab_ref, var_ref, q_ref, k_ref, v_ref, kc_ref, vc_ref, bias_ref, o_ref, *, kr, hp):
    del var_ref
    j = pl.program_id(2)
    ks = pl.multiple_of(slab_ref[j] * GRID_W, GRID_W)
    nk = kr * GRID_W
    hd = HEAD_DIM
    scale = hd ** -0.5
    for hh in range(hp):
        cols = slice(hh * hd, (hh + 1) * hd)
        q = q_ref[:, cols]
        s_loc = _nt_dot(q, k_ref[pl.ds(ks, nk), cols]) * scale + bias_ref[hh]
        s_ctx = _nt_dot(q, kc_ref[:, cols]) * scale
        m = jnp.maximum(jnp.max(s_loc, axis=-1, keepdims=True), jnp.max(s_ctx, axis=-1, keepdims=True))
        p_loc = jnp.exp(s_loc - m)
        p_ctx = jnp.exp(s_ctx - m)
        v_ext = jnp.concatenate([v_ref[pl.ds(ks, nk), cols], jnp.ones((nk, hd), BF16)], axis=1)
        vc_ext = jnp.concatenate([vc_ref[:, cols], jnp.ones((vc_ref.shape[0], hd), BF16)], axis=1)
        oe = (jnp.dot(p_loc.astype(BF16), v_ext, preferred_element_type=F32)
              + jnp.dot(p_ctx.astype(BF16), vc_ext, preferred_element_type=F32))
        o_ref[:, cols] = (oe[:, :hd] / oe[:, hd:hd + 1]).astype(o_ref.dtype)


def _neighborhood_attn(p, pc, rpb, n_heads):
    b, t, _ = p.shape
    l = pc.shape[1]
    rows = t // GRID_W
    rb = min(NA_ROW_BLOCK, rows)
    hp = min(NA_HEADS_PER_STEP, n_heads)
    assert n_heads % hp == 0
    ng = n_heads // hp
    bias, kr, slab, variant_of = _na_bias(rpb, rows, rb)
    nblk = rows // rb
    tq = rb * GRID_W
    w = hp * HEAD_DIM
    grid_spec = pltpu.PrefetchScalarGridSpec(
        num_scalar_prefetch=2,
        grid=(b, ng, nblk),
        in_specs=[
            pl.BlockSpec((None, tq, w), lambda i, h, j, s, v: (i, j, h)),
            pl.BlockSpec((None, t, w), lambda i, h, j, s, v: (i, 0, ng + h)),
            pl.BlockSpec((None, t, w), lambda i, h, j, s, v: (i, 0, 2 * ng + h)),
            pl.BlockSpec((None, l, w), lambda i, h, j, s, v: (i, 0, ng + h)),
            pl.BlockSpec((None, l, w), lambda i, h, j, s, v: (i, 0, 2 * ng + h)),
            pl.BlockSpec((hp, None, tq, kr * GRID_W), lambda i, h, j, s, v: (h, v[j], 0, 0)),
        ],
        out_specs=pl.BlockSpec((None, tq, w), lambda i, h, j, s, v: (i, j, h)),
    )
    return pl.pallas_call(
        functools.partial(_na_kernel, kr=kr, hp=hp),
        grid_spec=grid_spec,
        out_shape=jax.ShapeDtypeStruct((b, t, n_heads * HEAD_DIM), BF16),
        compiler_params=_params(3),
        name="neighborhood_attn",
    )(jnp.asarray(slab, jnp.int32), jnp.asarray(variant_of, jnp.int32), p, p, p, pc, pc, bias)


def _rope_tables(s):
    tpos = jnp.arange(s)
    row = (tpos // GRID_W).astype(F32)
    col = (tpos % GRID_W).astype(F32)
    nf = HEAD_DIM // 4
    inv = ROPE_THETA ** (-jnp.arange(nf, dtype=F32) / nf)
    lane = np.arange(HEAD_DIM)
    axis = lane // (2 * nf)
    second = (lane % (2 * nf)) >= nf
    freq = lane % nf
    pos = jnp.where(jnp.asarray(axis)[None, :] == 0, row[:, None], col[:, None])
    ang = pos * inv[jnp.asarray(freq)][None, :]
    cos, sin = jnp.cos(ang), jnp.sin(ang)
    sec = jnp.asarray(second)[None, :]
    return cos, jnp.where(sec, 0.0, -sin), jnp.where(sec, sin, 0.0)


def _rope(x, c, s1, s2):
    nf = HEAD_DIM // 4
    return x * c + pltpu.roll(x, HEAD_DIM - nf, axis=1) * s1 + pltpu.roll(x, nf, axis=1) * s2


def _wa_kernel(sink_ref, q_ref, k_ref, v_ref, kc_ref, vc_ref, c_ref, s1_ref, s2_ref, o_ref,
               *, group, kp, seq):
    kg = pl.program_id(1)
    n = pl.program_id(2)
    wb, hd = WA_BLOCK, HEAD_DIM
    nl = 3 * wb
    q0 = pl.multiple_of(n * wb, wb)
    bs = pl.multiple_of(jnp.clip(n * wb - wb, 0, seq - nl), wb)
    scale = hd ** -0.5
    cq, s1q, s2q = c_ref[pl.ds(q0, wb), :], s1_ref[pl.ds(q0, wb), :], s2_ref[pl.ds(q0, wb), :]
    ck, s1k, s2k = c_ref[pl.ds(bs, nl), :], s1_ref[pl.ds(bs, nl), :], s2_ref[pl.ds(bs, nl), :]
    rowi = lax.broadcasted_iota(jnp.int32, (group * wb, nl), 0)
    kpos = bs + lax.broadcasted_iota(jnp.int32, (group * wb, nl), 1)
    qpos = q0 + (rowi & (wb - 1))
    in_window = jnp.abs(kpos - qpos) <= WA_WINDOW
    gi = lax.broadcasted_iota(jnp.int32, (group * wb, 1), 0) // wb
    for kk in range(kp):
        kcols = slice(kk * hd, (kk + 1) * hd)
        q = jnp.concatenate(
            [_rope(q_ref[:, (kk * group + g) * hd:(kk * group + g + 1) * hd].astype(F32), cq, s1q, s2q)
             .astype(BF16) for g in range(group)], axis=0)
        kb = _rope(k_ref[pl.ds(bs, nl), kcols].astype(F32), ck, s1k, s2k).astype(BF16)
        s_loc = jnp.where(in_window, _nt_dot(q, kb) * scale, NEG_INF)
        s_ctx = _nt_dot(q, kc_ref[:, kcols]) * scale
        sink = jnp.zeros((group * wb, 1), F32)
        for g in range(group):
            sink = jnp.where(gi == g, sink_ref[(kg * kp + kk) * group + g], sink)
        m = jnp.maximum(jnp.maximum(jnp.max(s_loc, axis=-1, keepdims=True),
                                    jnp.max(s_ctx, axis=-1, keepdims=True)), sink)
        p_loc = jnp.exp(s_loc - m)
        p_ctx = jnp.exp(s_ctx - m)
        den = (jnp.sum(p_loc, axis=-1, keepdims=True) + jnp.sum(p_ctx, axis=-1, keepdims=True)
               + jnp.exp(sink - m))
        o = (jnp.dot(p_loc.astype(BF16), v_ref[pl.ds(bs, nl), kcols], preferred_element_type=F32)
             + jnp.dot(p_ctx.astype(BF16), vc_ref[:, kcols], preferred_element_type=F32))
        o = (o / den).astype(o_ref.dtype)
        for g in range(group):
            o_ref[:, (kk * group + g) * hd:(kk * group + g + 1) * hd] = o[g * wb:(g + 1) * wb, :]


def _window_gqa(p, pc, sink, q_col, k_col, v_col, n_heads, rope):
    b, t, _ = p.shape
    l = pc.shape[1]
    kvh = WA_KV_HEADS
    group = n_heads // kvh
    kp = min(WA_KV_PER_STEP, kvh)
    hd, wb = HEAD_DIM, WA_BLOCK
    qw = kp * group * hd
    kw = kp * hd
    assert kvh % kp == 0 and q_col % qw == 0 and k_col % kw == 0 and v_col % kw == 0
    assert t % wb == 0 and t >= 3 * wb
    cos, s1, s2 = rope
    tab_spec = pl.BlockSpec((t, hd), lambda i, k, n: (0, 0))
    return pl.pallas_call(
        functools.partial(_wa_kernel, group=group, kp=kp, seq=t),
        grid=(b, kvh // kp, t // wb),
        in_specs=[
            pl.BlockSpec(memory_space=pltpu.SMEM),
            pl.BlockSpec((None, wb, qw), lambda i, k, n: (i, n, q_col // qw + k)),
            pl.BlockSpec((None, t, kw), lambda i, k, n: (i, 0, k_col // kw + k)),
            pl.BlockSpec((None, t, kw), lambda i, k, n: (i, 0, v_col // kw + k)),
            pl.BlockSpec((None, l, kw), lambda i, k, n: (i, 0, k_col // kw + k)),
            pl.BlockSpec((None, l, kw), lambda i, k, n: (i, 0, v_col // kw + k)),
            tab_spec, tab_spec, tab_spec,
        ],
        out_specs=pl.BlockSpec((None, wb, qw), lambda i, k, n: (i, n, k)),
        out_shape=jax.ShapeDtypeStruct((b, t, n_heads * hd), BF16),
        compiler_params=_params(3),
        name="window_gqa",
    )(sink, p, p, p, pc, pc, cos, s1, s2)


def _ctx_kernel(sink_ref, q_ref, *refs, hp):
    k_refs, v_refs, o_ref = refs[:hp], refs[hp:2 * hp], refs[2 * hp]
    hd = HEAD_DIM
    scale = hd ** -0.5
    for hh in range(hp):
        cols = slice(hh * hd, (hh + 1) * hd)
        s = _nt_dot(q_ref[:, cols], k_refs[hh][...]) * scale
        sink = sink_ref[pl.program_id(1) * hp + hh]
        m = jnp.maximum(jnp.max(s, axis=-1, keepdims=True), sink)
        p = jnp.exp(s - m)
        den = jnp.sum(p, axis=-1, keepdims=True) + jnp.exp(sink - m)
        o = jnp.dot(p.astype(BF16), v_refs[hh][...], preferred_element_type=F32)
        o_ref[:, cols] = (o / den).astype(o_ref.dtype)


def _ctx_attn(pc, sink, na_heads, wa_heads):
    b, l, _ = pc.shape
    hd = HEAD_DIM
    group = wa_heads // WA_KV_HEADS
    hp = CTX_HEADS_PER_STEP
    while na_heads % hp or wa_heads % hp:
        hp //= 2
    qb0 = 3 * na_heads
    kb0 = qb0 + wa_heads
    vb0 = kb0 + WA_KV_HEADS
    sink_all = jnp.concatenate([jnp.full((na_heads,), NEG_INF, F32), sink.astype(F32)])

    def qmap(i, g):
        return (i, 0, jnp.where(g * hp < na_heads, g, (qb0 - na_heads) // hp + g))

    def kv_spec(hh, mha0, gqa0):
        def index(i, g):
            h = g * hp + hh
            return (i, 0, jnp.where(h < na_heads, mha0 + h, gqa0 + (h - na_heads) // group))
        return pl.BlockSpec((None, l, hd), index)

    return pl.pallas_call(
        functools.partial(_ctx_kernel, hp=hp),
        grid=(b, (na_heads + wa_heads) // hp),
        in_specs=[pl.BlockSpec(memory_space=pltpu.SMEM), pl.BlockSpec((None, l, hp * hd), qmap)]
        + [kv_spec(hh, na_heads, kb0) for hh in range(hp)]
        + [kv_spec(hh, 2 * na_heads, vb0) for hh in range(hp)],
        out_specs=pl.BlockSpec((None, l, hp * hd), lambda i, g: (i, 0, g)),
        out_shape=jax.ShapeDtypeStruct((b, l, (na_heads + wa_heads) * hd), BF16),
        compiler_params=_params(2),
        name="context_attn",
    )(sink_all, pc, *([pc] * (2 * hp)))


DFT_RADIX = 4
def _dft_cos_sin(n, ks, nt):
    lo = 1
    while lo * lo < nt:
        lo *= 2
    hi = nt // lo
    k = jnp.asarray(ks, dtype=jnp.int32)
    a = (2.0 * np.pi / n) * ((k[:, None] * (jnp.arange(hi, dtype=jnp.int32) * lo)[None, :]) % n).astype(F32)
    bb = (2.0 * np.pi / n) * ((k[:, None] * jnp.arange(lo, dtype=jnp.int32)[None, :]) % n).astype(F32)
    ca, sa, cb, sb = jnp.cos(a)[:, :, None], jnp.sin(a)[:, :, None], jnp.cos(bb)[:, None, :], jnp.sin(bb)[:, None, :]
    return (ca * cb - sa * sb).reshape(len(ks), nt), (sa * cb + ca * sb).reshape(len(ks), nt)


def _group_dft_kernel(u0_ref, u1_ref, u2_ref, u3_ref, cs_ref, o_ref, *, groups):
    gd = FNET_GROUP_DIM
    cs = cs_ref[...]
    for g in range(groups):
        cols = slice(g * gd, (g + 1) * gd)
        z = [jnp.dot(u_ref[:, cols], cs, preferred_element_type=F32) for u_ref in (u0_ref, u1_ref, u2_ref, u3_ref)]
        (a0, b0), (a1, b1), (a2, b2), (a3, b3) = [(zq[:, :gd], zq[:, gd:]) for zq in z]
        w_parts = [
            (a0 + a1 + a2 + a3, b0 + b1 + b2 + b3),
            (a0 - b1 - a2 + b3, b0 + a1 - b2 - a3),
            (a0 - a1 + a2 - a3, b0 - b1 + b2 - b3),
            (a0 + b1 - a2 - b3, b0 - a1 - b2 + a3),
        ]
        for r, (re, im) in enumerate(w_parts):
            o_ref[r, 0, :, cols] = re.astype(o_ref.dtype)
            o_ref[r, 1, :, cols] = im.astype(o_ref.dtype)


def _group_dft(p, width):
    b, t, _ = p.shape
    gd = FNET_GROUP_DIM
    c, s = _dft_cos_sin(gd, np.arange(gd), gd)
    cs = (jnp.concatenate([c, s], axis=1) * (gd ** -0.5)).astype(BF16)
    tq = t // DFT_RADIX
    tt = _pick(tq, 256)
    nq = tq // tt

    def rows(q):
        return pl.BlockSpec((None, tt, width), lambda i, j: (i, q * nq + j, 0))

    return pl.pallas_call(
        functools.partial(_group_dft_kernel, groups=width // gd),
        grid=(b, nq),
        in_specs=[rows(q) for q in range(DFT_RADIX)] + [pl.BlockSpec((gd, 2 * gd), lambda i, j: (0, 0))],
        out_specs=pl.BlockSpec((None, DFT_RADIX, 2, tt, width), lambda i, j: (i, 0, 0, j, 0)),
        out_shape=jax.ShapeDtypeStruct((b, DFT_RADIX, 2, tq, width), BF16),
        compiler_params=_params(2),
        name="group_dft",
    )(p, p, p, p, cs)


def _seq_dft(v):
    b, radix, _, tq, w = v.shape
    t = radix * tq
    mats = []
    for r in range(radix):
        c, s = _dft_cos_sin(t, np.arange(r, t, radix), tq)
        mats.append(jnp.concatenate([c, -s], axis=1) * (t ** -0.5))
    a = jnp.stack(mats).astype(BF16)
    tm = _pick(tq, 1024)
    tn = _pick(w, 1024)
    nj = w // tn
    out = pl.pallas_call(
        functools.partial(_mm_kernel, n_parts=1),
        grid=(b, radix, nj, tq // tm),
        in_specs=[pl.BlockSpec((None, tm, 2 * tq), lambda g, r, j, i: (r, i, 0)),
                  pl.BlockSpec((None, None, 2 * tq, tn), lambda g, r, j, i: (g, r, 0, j))],
        out_specs=pl.BlockSpec((None, tm, tn), lambda g, r, j, i: (g, i, r * nj + j)),
        out_shape=jax.ShapeDtypeStruct((b, tq, radix * w), BF16),
        compiler_params=_params(4),
        name="seq_dft",
    )(a, v.reshape(b, radix, 2 * tq, w))
    return out.reshape(b, t, w)


def _conv_cols(t, width):
    tc = LANES
    while tc * 2 * t <= CONV_BLOCK_ELEMS and width % (tc * 2) == 0:
        tc *= 2
    return tc


def _dwconv3(z, w_ref):
    t = z.shape[0]
    sub = min(F32_SUBLANES, t)
    row = lax.broadcasted_iota(jnp.int32, (sub, z.shape[1]), 0)
    zp = pltpu.roll(z, 1, axis=0)
    zn = pltpu.roll(z, t - 1, axis=0)
    zp = jnp.concatenate([jnp.where(row == 0, 0.0, zp[:sub]), zp[sub:]], axis=0)
    zn = jnp.concatenate([zn[:t - sub], jnp.where(row == sub - 1, 0.0, zn[t - sub:])], axis=0)
    return zp * w_ref[0:1, :] + z * w_ref[1:2, :] + zn * w_ref[2:3, :]


def _shortconv_kernel(bg_ref, cg_ref, hv_ref, w_ref, o_ref):
    z = cg_ref[...].astype(F32) * hv_ref[...].astype(F32)
    o_ref[...] = (bg_ref[...].astype(F32) * _dwconv3(z, w_ref)).astype(o_ref.dtype)


def _shortconv(p, conv_w, layer, col0, width):
    b, t, _ = p.shape
    tc = _conv_cols(t, math.gcd(width, col0))
    nb = width // tc
    c0 = col0 // tc
    return pl.pallas_call(
        _shortconv_kernel,
        grid=(b, nb),
        in_specs=[pl.BlockSpec((None, t, tc), lambda i, j: (i, 0, c0 + j)),
                  pl.BlockSpec((None, t, tc), lambda i, j: (i, 0, c0 + nb + j)),
                  pl.BlockSpec((None, t, tc), lambda i, j: (i, 0, c0 + 2 * nb + j)),
                  pl.BlockSpec((None, 3, tc), lambda i, j: (layer, 0, j))],
        out_specs=pl.BlockSpec((None, t, tc), lambda i, j: (i, 0, j)),
        out_shape=jax.ShapeDtypeStruct((b, t, width), BF16),
        compiler_params=_params(2),
        name="shortconv_gate",
    )(p, p, p, conv_w)


def _ffn_gate_kernel(g_ref, v_ref, wg_ref, wv_ref, o_ref):
    gate = _dwconv3(g_ref[...].astype(F32), wg_ref)
    val = _dwconv3(v_ref[...].astype(F32), wv_ref)
    o_ref[...] = (gate / (1.0 + jnp.exp(-gate)) * val).astype(o_ref.dtype)


def _ffn_gate(u, conv_w, layer):
    b, t, f2 = u.shape
    f = f2 // 2
    tc = _conv_cols(t, f)
    nb = f // tc
    return pl.pallas_call(
        _ffn_gate_kernel,
        grid=(b, nb),
        in_specs=[pl.BlockSpec((None, t, tc), lambda i, j: (i, 0, j)),
                  pl.BlockSpec((None, t, tc), lambda i, j: (i, 0, nb + j)),
                  pl.BlockSpec((None, 3, tc), lambda i, j: (layer, 0, j)),
                  pl.BlockSpec((None, 3, tc), lambda i, j: (layer, 0, nb + j))],
        out_specs=pl.BlockSpec((None, t, tc), lambda i, j: (i, 0, j)),
        out_shape=jax.ShapeDtypeStruct((b, t, f), BF16),
        compiler_params=_params(2),
        name="ffn_conv_gate",
    )(u, u, conv_w, conv_w)


def _proj(parts, w, layer, out_dtype, **kw):
    b, t, _ = parts[0].shape
    y = _matmul([a.reshape(b * t, a.shape[2]) for a in parts], w, layer, out_dtype, **kw)
    return y.reshape(b, t, -1)


def _even_mixer(hl, hc, w_in, rpb, sink, w_out, e, rope, ctx_live):
    na_heads = rpb.shape[1]
    wa_heads = sink.shape[1]
    na_w = na_heads * HEAD_DIM
    wa_w = wa_heads * HEAD_DIM
    kv_w = WA_KV_HEADS * HEAD_DIM
    p = _proj([hl], w_in, e, BF16, name="even_w_in")
    pc = _proj([hc], w_in, e, BF16, name="even_w_in_ctx")
    oa = _neighborhood_attn(p, pc, rpb[e], na_heads)
    ob = _window_gqa(p, pc, sink[e].astype(F32), 3 * na_w, 3 * na_w + wa_w, 3 * na_w + wa_w + kv_w,
                     wa_heads, rope)
    yl = _proj([oa, ob], w_out, e, BF16, name="even_w_out")
    yc = None
    if ctx_live:
        yc = _proj([_ctx_attn(pc, sink[e], na_heads, wa_heads)], w_out, e, BF16, name="even_w_out_ctx")
    return yl, yc


def _odd_mixer(h, w_in, conv_w, w_out, o):
    sc_w = conv_w.shape[2]
    f_w = w_in.shape[2] - 3 * sc_w
    p = _proj([h], w_in, o, BF16, name="odd_w_in")
    f = _seq_dft(_group_dft(p, f_w))
    sc = _shortconv(p, conv_w, o, f_w, sc_w)
    return _proj([f, sc], w_out, o, BF16, name="odd_w_out")


def _conv_ffn(h, w_up, conv_w, w_down, i):
    u = _proj([h], w_up, i, BF16, name="ffn_w_up")
    a = _ffn_gate(u, conv_w, i)
    return _proj([a], w_down, i, BF16, tm=512, tn=1024, name="ffn_w_down")


def kernel(x, c, ctx, c_ctx, w_mod, b_mod, g_mix_pre, g_mix_post, g_ffn_pre, g_ffn_post, even_w_in, even_rpb, even_sink, even_w_out, odd_w_in, odd_conv, odd_w_out, ffn_w_up, ffn_conv, ffn_w_down):
    b, t, d = x.shape
    depth = w_mod.shape[0]
    rows = -(-(b + 1) // 8) * 8
    cond = jnp.zeros((rows, d), F32).at[:b].set(c).at[b].set(c_ctx)
    mod = _modulation(cond, w_mod, b_mod)
    rope = _rope_tables(t)
    xc = ctx
    hl = _prenorm(x, g_mix_pre[0], mod, 0, 0)
    for i in range(depth):
        ctx_live = any(j % 2 == 0 for j in range(i + 1, depth))
        need_hc = (i % 2 == 0) or ctx_live
        hc = _prenorm(xc, g_mix_pre[i], mod, i, 0, fixed_row=b) if need_hc else None
        if i % 2 == 0:
            e = i // 2
            yl, yc = _even_mixer(hl, hc, even_w_in, even_rpb, even_sink, even_w_out, e, rope, ctx_live)
        else:
            o = i // 2
            yl = _odd_mixer(hl, odd_w_in, odd_conv, odd_w_out, o)
            yc = _odd_mixer(hc, odd_w_in, odd_conv, odd_w_out, o) if ctx_live else None
        x, hl = _residual(x, yl, g_mix_post[i], mod, i, 2, nxt=(g_ffn_pre[i], i, 3))
        yf = _conv_ffn(hl, ffn_w_up, ffn_conv, ffn_w_down, i)
        nxt = (g_mix_pre[i + 1], i + 1, 0) if i + 1 < depth else None
        x, hl = _residual(x, yf, g_ffn_post[i], mod, i, 5, nxt=nxt)
        if ctx_live:
            xc, hcf = _residual(xc, yc, g_mix_post[i], mod, i, 2, nxt=(g_ffn_pre[i], i, 3), fixed_row=b)
            yfc = _conv_ffn(hcf, ffn_w_up, ffn_conv, ffn_w_down, i)
            xc, _ = _residual(xc, yfc, g_ffn_post[i], mod, i, 5, fixed_row=b)
    return x
```

```python
import functools
import math

import numpy as np
import jax
import jax.numpy as jnp
from jax import lax
from jax.experimental import pallas as pl
from jax.experimental.pallas import tpu as pltpu

F32 = jnp.float32
BF16 = jnp.bfloat16

GRID_W = 64
HEAD_DIM = 128
NA_KH = 8
NA_KW = 16
WA_KV_HEADS = 4
WA_WINDOW = 128
WA_BLOCK = 128
FNET_GROUP_DIM = 128
ROPE_THETA = 10000.0
RMS_EPS = 1e-6
NEG_INF = -1e30
LOG2E = 1.4426950408889634

V7X_VMEM_BYTES = 64 * 1024 * 1024
VMEM_LIMIT = V7X_VMEM_BYTES - 8 * 1024 * 1024
LANES = 128

NA_ROW_BLOCK = 4
NA_HEADS_PER_STEP = 8
WA_KV_PER_STEP = 4
CTX_HEADS_PER_STEP = 4
CONV_BLOCK_ELEMS = 4096 * 128
F32_SUBLANES = 8
MIN_ROW_TILES_FOR_STAGING = 8


def _params(n_grid):
    return pltpu.CompilerParams(dimension_semantics=("arbitrary",) * n_grid,
                                vmem_limit_bytes=VMEM_LIMIT)


def _pick(n, pref):
    if n <= pref:
        return n
    t = pref
    while n % t:
        t //= 2
    return t


def _mod_kernel(c_ref, w_ref, b_ref, o_ref):
    c = c_ref[...]
    s = (c / (1.0 + jnp.exp(-c))).astype(BF16)
    y = jnp.dot(s, w_ref[...].astype(BF16), preferred_element_type=F32)
    o_ref[...] = y + b_ref[...]


def _modulation(cond, w_mod, b_mod):
    depth, d, n = w_mod.shape
    rows = cond.shape[0]
    tn = _pick(n, 512)
    return pl.pallas_call(
        _mod_kernel,
        grid=(depth, n // tn),
        in_specs=[pl.BlockSpec((rows, d), lambda l, j: (0, 0)),
                  pl.BlockSpec((None, d, tn), lambda l, j: (l, 0, j)),
                  pl.BlockSpec((None, 1, tn), lambda l, j: (l, 0, j))],
        out_specs=pl.BlockSpec((None, rows, tn), lambda l, j: (l, 0, j)),
        out_shape=jax.ShapeDtypeStruct((depth, rows, n), F32),
        compiler_params=_params(2),
        name="adaln_mod",
    )(cond, w_mod, b_mod.reshape(depth, 1, n))


def _rms(v, g):
    return v * lax.rsqrt(jnp.mean(v * v, axis=-1, keepdims=True) + RMS_EPS) * g


def _mod_row(ref, fixed_row):
    row = pl.program_id(0) if fixed_row is None else fixed_row
    return ref[pl.ds(row, 1), :]


def _prenorm_kernel(x_ref, g_ref, sh_ref, sc_ref, h_ref, *, fixed_row):
    sh = _mod_row(sh_ref, fixed_row)
    sc = _mod_row(sc_ref, fixed_row)
    h_ref[...] = (_rms(x_ref[...], g_ref[...]) * (1.0 + sc) + sh).astype(h_ref.dtype)


def _prenorm(x, g, mod, layer, slot, fixed_row=None):
    b, t, d = x.shape
    rows = mod.shape[1]
    tr = _pick(t, 256)
    return pl.pallas_call(
        functools.partial(_prenorm_kernel, fixed_row=fixed_row),
        grid=(b, t // tr),
        in_specs=[pl.BlockSpec((None, tr, d), lambda i, j: (i, j, 0)),
                  pl.BlockSpec((1, d), lambda i, j: (0, 0)),
                  pl.BlockSpec((None, rows, d), lambda i, j: (layer, 0, slot)),
                  pl.BlockSpec((None, rows, d), lambda i, j: (layer, 0, slot + 1))],
        out_specs=pl.BlockSpec((None, tr, d), lambda i, j: (i, j, 0)),
        out_shape=jax.ShapeDtypeStruct((b, t, d), BF16),
        compiler_params=_params(2),
        name="prenorm",
    )(x, g.reshape(1, d), mod, mod)


def _resid_kernel(x_ref, y_ref, gp_ref, gt_ref, *rest, fixed_row, with_next):
    gt = _mod_row(gt_ref, fixed_row)
    xn = x_ref[...] + gt * _rms(y_ref[...].astype(F32), gp_ref[...])
    if with_next:
        gn_ref, sh_ref, sc_ref, xo_ref, h_ref = rest
        sh = _mod_row(sh_ref, fixed_row)
        sc = _mod_row(sc_ref, fixed_row)
        h_ref[...] = (_rms(xn, gn_ref[...]) * (1.0 + sc) + sh).astype(h_ref.dtype)
    else:
        (xo_ref,) = rest
    xo_ref[...] = xn


def _residual(x, y, g_post, mod, layer, gate_slot, nxt=None, fixed_row=None):
    b, t, d = x.shape
    rows = mod.shape[1]
    tr = _pick(t, 256)
    row_spec = pl.BlockSpec((None, tr, d), lambda i, j: (i, j, 0))
    vec_spec = pl.BlockSpec((1, d), lambda i, j: (0, 0))

    def mod_spec(l, s):
        return pl.BlockSpec((None, rows, d), lambda i, j: (l, 0, s))

    in_specs = [row_spec, row_spec, vec_spec, mod_spec(layer, gate_slot)]
    args = [x, y, g_post.reshape(1, d), mod]
    out_specs = [row_spec]
    out_shape = [jax.ShapeDtypeStruct((b, t, d), F32)]
    if nxt is not None:
        g_pre, nl, ns = nxt
        in_specs += [vec_spec, mod_spec(nl, ns), mod_spec(nl, ns + 1)]
        args += [g_pre.reshape(1, d), mod, mod]
        out_specs.append(row_spec)
        out_shape.append(jax.ShapeDtypeStruct((b, t, d), BF16))
    out = pl.pallas_call(
        functools.partial(_resid_kernel, fixed_row=fixed_row, with_next=nxt is not None),
        grid=(b, t // tr),
        in_specs=in_specs, out_specs=out_specs, out_shape=out_shape,
        compiler_params=_params(2),
        name="residual_norm",
    )(*args)
    return (out[0], out[1]) if nxt is not None else (out[0], None)


def _dot_parts(a_refs, w_ref, o_ref):
    acc = None
    k0 = 0
    for a_ref in a_refs:
        kp = a_ref.shape[1]
        part = jnp.dot(a_ref[...], w_ref[k0:k0 + kp, :], preferred_element_type=F32)
        acc = part if acc is None else acc + part
        k0 += kp
    o_ref[...] = acc.astype(o_ref.dtype)


def _mm_kernel(*refs, n_parts):
    _dot_parts(refs[:n_parts], refs[n_parts], refs[n_parts + 1])


def _mm_cast_kernel(*refs, n_parts):
    a_refs = refs[:n_parts]
    w_ref, o_ref, wbf = refs[n_parts:]

    @pl.when(pl.program_id(1) == 0)
    def _():
        wbf[...] = w_ref[...].astype(BF16)

    _dot_parts(a_refs, wbf, o_ref)


def _mm_staged_kernel(*refs, n_parts, layer, tn):
    a_refs = refs[:n_parts]
    w_hbm, o_ref, stage, wbf, sem = refs[n_parts:]
    j = pl.program_id(0)

    def tile_copy(jj):
        return pltpu.make_async_copy(w_hbm.at[layer, :, pl.ds(pl.multiple_of(jj * tn, tn), tn)], stage, sem)

    @pl.when(pl.program_id(1) == 0)
    def _():
        @pl.when(j == 0)
        def _():
            tile_copy(0).start()

        tile_copy(j).wait()
        wbf[...] = stage[...].astype(BF16)

        @pl.when(j + 1 < pl.num_programs(0))
        def _():
            tile_copy(j + 1).start()

    _dot_parts(a_refs, wbf, o_ref)


def _matmul(a_parts, w, layer, out_dtype, *, tm=1024, tn=1024, name="matmul"):
    m = a_parts[0].shape[0]
    _, k, n = w.shape
    assert sum(a.shape[1] for a in a_parts) == k
    tm = _pick(m, tm)
    n_parts = len(a_parts)
    a_specs = [pl.BlockSpec((tm, a.shape[1]), lambda j, i: (i, 0)) for a in a_parts]
    out_shape = jax.ShapeDtypeStruct((m, n), out_dtype)
    if m // tm >= MIN_ROW_TILES_FOR_STAGING:
        tn = _pick(n, tn)
        return pl.pallas_call(
            functools.partial(_mm_staged_kernel, n_parts=n_parts, layer=layer, tn=tn),
            grid=(n // tn, m // tm),
            in_specs=a_specs + [pl.BlockSpec(memory_space=pl.ANY)],
            out_specs=pl.BlockSpec((tm, tn), lambda j, i: (i, j)),
            out_shape=out_shape,
            scratch_shapes=[pltpu.VMEM((k, tn), F32), pltpu.VMEM((k, tn), BF16), pltpu.SemaphoreType.DMA(())],
            compiler_params=_params(2),
            name=name,
        )(*a_parts, w)
    tn = _pick(n, tn // 2)
    return pl.pallas_call(
        functools.partial(_mm_cast_kernel, n_parts=n_parts),
        grid=(n // tn, m // tm),
        in_specs=a_specs + [pl.BlockSpec((None, k, tn), lambda j, i: (layer, 0, j))],
        out_specs=pl.BlockSpec((tm, tn), lambda j, i: (i, j)),
        out_shape=out_shape,
        scratch_shapes=[pltpu.VMEM((k, tn), BF16)],
        compiler_params=_params(2),
        name=name,
    )(*a_parts, w)


def _na_geometry(rows, rb):
    kh = min(NA_KH, rows)
    kr = min(rb + kh, rows)
    nblk = rows // rb
    slab = np.clip(np.arange(nblk) * rb - kh // 2, 0, rows - kr)
    variants, variant_of = [], []
    for j in range(nblk):
        geo = []
        for a in range(rb):
            r = j * rb + a
            rs = int(np.clip(r - kh // 2, 0, rows - kh))
            lo = rs - int(slab[j])
            geo.append((lo, lo + kh, rs - r + NA_KH - 1))
        geo = tuple(geo)
        if geo not in variants:
            variants.append(geo)
        variant_of.append(variants.index(geo))
    return kr, slab, np.asarray(variant_of), variants


def _na_bias_table(rpb):
    h = rpb.shape[0]
    qcol = np.arange(GRID_W)
    cstart = np.clip(qcol - NA_KW // 2, 0, GRID_W - NA_KW)
    kcol = np.arange(GRID_W)
    col_ok = (kcol[None, :] >= cstart[:, None]) & (kcol[None, :] < cstart[:, None] + NA_KW)
    col_rel = np.clip(kcol[None, :] - qcol[:, None], -(NA_KW - 1), NA_KW - 1) + NA_KW - 1
    e = jnp.take(rpb, jnp.asarray(col_rel.reshape(-1)), axis=2).reshape(h, 2 * NA_KH - 1, GRID_W, GRID_W)
    e = jnp.where(jnp.asarray(col_ok)[None, None], e, NEG_INF)
    e = e.transpose(0, 2, 1, 3).reshape(h, GRID_W, (2 * NA_KH - 1) * GRID_W)
    return jnp.pad(e, ((0, 0), (0, 0), (0, GRID_W)), constant_values=NEG_INF).astype(F32)


def _nt_dot(a, b):
    return lax.dot_general(a, b, (((1,), (1,)), ((), ())), preferred_element_type=F32)


def _na_kernel(slab_ref, var_ref, q_ref, k_ref, v_ref, kc_ref, vc_ref, e_ref, o_ref, bias_ref,
               *, kr, hp, variants):
    j = pl.program_id(2)
    gw = GRID_W
    ks = pl.multiple_of(slab_ref[j] * gw, gw)
    nk = kr * gw
    hd = HEAD_DIM
    scale = hd ** -0.5

    changed = (j == 0) | (var_ref[j] != var_ref[jnp.maximum(j - 1, 0)])
    for vi, geo in enumerate(variants):
        @pl.when(changed & (var_ref[j] == vi))
        def _(geo=geo):
            for hh in range(hp):
                for a, (lo, hi, rr0) in enumerate(geo):
                    qrows = slice(a * gw, (a + 1) * gw)
                    if lo > 0:
                        bias_ref[hh, qrows, 0:lo * gw] = jnp.full((gw, lo * gw), NEG_INF, F32)
                    bias_ref[hh, qrows, lo * gw:hi * gw] = e_ref[hh, :, rr0 * gw:(rr0 + hi - lo) * gw]
                    if hi < kr:
                        bias_ref[hh, qrows, hi * gw:] = jnp.full((gw, (kr - hi) * gw), NEG_INF, F32)

    for hh in range(hp):
        cols = slice(hh * hd, (hh + 1) * hd)
        q = q_ref[:, cols]
        s_loc = _nt_dot(q, k_ref[pl.ds(ks, nk), cols]) * scale + bias_ref[hh]
        s_ctx = _nt_dot(q, kc_ref[:, cols]) * scale
        m = jnp.maximum(jnp.max(s_loc, axis=-1, keepdims=True), jnp.max(s_ctx, axis=-1, keepdims=True))
        p_loc = jnp.exp(s_loc - m)
        p_ctx = jnp.exp(s_ctx - m)
        v_ext = jnp.concatenate([v_ref[pl.ds(ks, nk), cols], jnp.ones((nk, hd), BF16)], axis=1)
        vc_ext = jnp.concatenate([vc_ref[:, cols], jnp.ones((vc_ref.shape[0], hd), BF16)], axis=1)
        oe = (jnp.dot(p_loc.astype(BF16), v_ext, preferred_element_type=F32)
              + jnp.dot(p_ctx.astype(BF16), vc_ext, preferred_element_type=F32))
        o_ref[:, cols] = (oe[:, :hd] / oe[:, hd:hd + 1]).astype(o_ref.dtype)


def _neighborhood_attn(p, pc, rpb, n_heads):
    b, t, _ = p.shape
    l = pc.shape[1]
    rows = t // GRID_W
    rb = min(NA_ROW_BLOCK, rows)
    hp = min(NA_HEADS_PER_STEP, n_heads)
    assert n_heads % hp == 0
    ng = n_heads // hp
    kr, slab, variant_of, variants = _na_geometry(rows, rb)
    table = _na_bias_table(rpb)
    nblk = rows // rb
    tq = rb * GRID_W
    w = hp * HEAD_DIM
    grid_spec = pltpu.PrefetchScalarGridSpec(
        num_scalar_prefetch=2,
        grid=(b, ng, nblk),
        in_specs=[
            pl.BlockSpec((None, tq, w), lambda i, h, j, s, v: (i, j, h)),
            pl.BlockSpec((None, t, w), lambda i, h, j, s, v: (i, 0, ng + h)),
            pl.BlockSpec((None, t, w), lambda i, h, j, s, v: (i, 0, 2 * ng + h)),
            pl.BlockSpec((None, l, w), lambda i, h, j, s, v: (i, 0, ng + h)),
            pl.BlockSpec((None, l, w), lambda i, h, j, s, v: (i, 0, 2 * ng + h)),
            pl.BlockSpec((hp, GRID_W, table.shape[2]), lambda i, h, j, s, v: (h, 0, 0)),
        ],
        out_specs=pl.BlockSpec((None, tq, w), lambda i, h, j, s, v: (i, j, h)),
        scratch_shapes=[pltpu.VMEM((hp, tq, kr * GRID_W), F32)],
    )
    return pl.pallas_call(
        functools.partial(_na_kernel, kr=kr, hp=hp, variants=tuple(variants)),
        grid_spec=grid_spec,
        out_shape=jax.ShapeDtypeStruct((b, t, n_heads * HEAD_DIM), BF16),
        compiler_params=_params(3),
        name="neighborhood_attn",
    )(jnp.asarray(slab, jnp.int32), jnp.asarray(variant_of, jnp.int32), p, p, p, pc, pc, table)


def _rope_tables(s):
    tpos = jnp.arange(s)
    row = (tpos // GRID_W).astype(F32)
    col = (tpos % GRID_W).astype(F32)
    nf = HEAD_DIM // 4
    inv = ROPE_THETA ** (-jnp.arange(nf, dtype=F32) / nf)
    lane = np.arange(HEAD_DIM)
    axis = lane // (2 * nf)
    second = (lane % (2 * nf)) >= nf
    freq = lane % nf
    pos = jnp.where(jnp.asarray(axis)[None, :] == 0, row[:, None], col[:, None])
    ang = pos * inv[jnp.asarray(freq)][None, :]
    cos, sin = jnp.cos(ang), jnp.sin(ang)
    sec = jnp.asarray(second)[None, :]
    return cos, jnp.where(sec, 0.0, -sin), jnp.where(sec, sin, 0.0)


def _rope(x, c, s1, s2):
    nf = HEAD_DIM // 4
    return x * c + pltpu.roll(x, HEAD_DIM - nf, axis=1) * s1 + pltpu.roll(x, nf, axis=1) * s2


def _wa_kernel(sink_ref, q_ref, k_ref, v_ref, kc_ref, vc_ref, c_ref, s1_ref, s2_ref, o_ref,
               *, group, kp, seq):
    kg = pl.program_id(1)
    n = pl.program_id(2)
    wb, hd = WA_BLOCK, HEAD_DIM
    nl = 3 * wb
    q0 = pl.multiple_of(n * wb, wb)
    bs = pl.multiple_of(jnp.clip(n * wb - wb, 0, seq - nl), wb)
    scale = hd ** -0.5
    cq, s1q, s2q = c_ref[pl.ds(q0, wb), :], s1_ref[pl.ds(q0, wb), :], s2_ref[pl.ds(q0, wb), :]
    ck, s1k, s2k = c_ref[pl.ds(bs, nl), :], s1_ref[pl.ds(bs, nl), :], s2_ref[pl.ds(bs, nl), :]
    rowi = lax.broadcasted_iota(jnp.int32, (group * wb, nl), 0)
    kpos = bs + lax.broadcasted_iota(jnp.int32, (group * wb, nl), 1)
    qpos = q0 + (rowi & (wb - 1))
    in_window = jnp.abs(kpos - qpos) <= WA_WINDOW
    gi = lax.broadcasted_iota(jnp.int32, (group * wb, 1), 0) // wb
    for kk in range(kp):
        kcols = slice(kk * hd, (kk + 1) * hd)
        q = jnp.concatenate(
            [_rope(q_ref[:, (kk * group + g) * hd:(kk * group + g + 1) * hd].astype(F32), cq, s1q, s2q)
             .astype(BF16) for g in range(group)], axis=0)
        kb = _rope(k_ref[pl.ds(bs, nl), kcols].astype(F32), ck, s1k, s2k).astype(BF16)
        s_loc = jnp.where(in_window, _nt_dot(q, kb) * scale, NEG_INF)
        s_ctx = _nt_dot(q, kc_ref[:, kcols]) * scale
        sink = jnp.zeros((group * wb, 1), F32)
        for g in range(group):
            sink = jnp.where(gi == g, sink_ref[(kg * kp + kk) * group + g], sink)
        m = jnp.maximum(jnp.maximum(jnp.max(s_loc, axis=-1, keepdims=True),
                                    jnp.max(s_ctx, axis=-1, keepdims=True)), sink)
        p_loc = jnp.exp(s_loc - m)
        p_ctx = jnp.exp(s_ctx - m)
        den = (jnp.sum(p_loc, axis=-1, keepdims=True) + jnp.sum(p_ctx, axis=-1, keepdims=True)
               + jnp.exp(sink - m))
        o = (jnp.dot(p_loc.astype(BF16), v_ref[pl.ds(bs, nl), kcols], preferred_element_type=F32)
             + jnp.dot(p_ctx.astype(BF16), vc_ref[:, kcols], preferred_element_type=F32))
        o = (o / den).astype(o_ref.dtype)
        for g in range(group):
            o_ref[:, (kk * group + g) * hd:(kk * group + g + 1) * hd] = o[g * wb:(g + 1) * wb, :]


def _window_gqa(p, pc, sink, q_col, k_col, v_col, n_heads, rope):
    b, t, _ = p.shape
    l = pc.shape[1]
    kvh = WA_KV_HEADS
    group = n_heads // kvh
    kp = min(WA_KV_PER_STEP, kvh)
    hd, wb = HEAD_DIM, WA_BLOCK
    qw = kp * group * hd
    kw = kp * hd
    assert kvh % kp == 0 and q_col % qw == 0 and k_col % kw == 0 and v_col % kw == 0
    assert t % wb == 0 and t >= 3 * wb
    cos, s1, s2 = rope
    tab_spec = pl.BlockSpec((t, hd), lambda i, k, n: (0, 0))
    return pl.pallas_call(
        functools.partial(_wa_kernel, group=group, kp=kp, seq=t),
        grid=(b, kvh // kp, t // wb),
        in_specs=[
            pl.BlockSpec(memory_space=pltpu.SMEM),
            pl.BlockSpec((None, wb, qw), lambda i, k, n: (i, n, q_col // qw + k)),
            pl.BlockSpec((None, t, kw), lambda i, k, n: (i, 0, k_col // kw + k)),
            pl.BlockSpec((None, t, kw), lambda i, k, n: (i, 0, v_col // kw + k)),
            pl.BlockSpec((None, l, kw), lambda i, k, n: (i, 0, k_col // kw + k)),
            pl.BlockSpec((None, l, kw), lambda i, k, n: (i, 0, v_col // kw + k)),
            tab_spec, tab_spec, tab_spec,
        ],
        out_specs=pl.BlockSpec((None, wb, qw), lambda i, k, n: (i, n, k)),
        out_shape=jax.ShapeDtypeStruct((b, t, n_heads * hd), BF16),
        compiler_params=_params(3),
        name="window_gqa",
    )(sink, p, p, p, pc, pc, cos, s1, s2)


def _ctx_kernel(sink_ref, q_ref, *refs, hp):
    k_refs, v_refs, o_ref = refs[:hp], refs[hp:2 * hp], refs[2 * hp]
    hd = HEAD_DIM
    scale = hd ** -0.5
    for hh in range(hp):
        cols = slice(hh * hd, (hh + 1) * hd)
        s = _nt_dot(q_ref[:, cols], k_refs[hh][...]) * scale
        sink = sink_ref[pl.program_id(1) * hp + hh]
        m = jnp.maximum(jnp.max(s, axis=-1, keepdims=True), sink)
        p = jnp.exp(s - m)
        den = jnp.sum(p, axis=-1, keepdims=True) + jnp.exp(sink - m)
        o = jnp.dot(p.astype(BF16), v_refs[hh][...], preferred_element_type=F32)
        o_ref[:, cols] = (o / den).astype(o_ref.dtype)


def _ctx_attn(pc, sink, na_heads, wa_heads):
    b, l, _ = pc.shape
    hd = HEAD_DIM
    group = wa_heads // WA_KV_HEADS
    hp = CTX_HEADS_PER_STEP
    while na_heads % hp or wa_heads % hp:
        hp //= 2
    qb0 = 3 * na_heads
    kb0 = qb0 + wa_heads
    vb0 = kb0 + WA_KV_HEADS
    sink_all = jnp.concatenate([jnp.full((na_heads,), NEG_INF, F32), sink.astype(F32)])

    def qmap(i, g):
        return (i, 0, jnp.where(g * hp < na_heads, g, (qb0 - na_heads) // hp + g))

    def kv_spec(hh, mha0, gqa0):
        def index(i, g):
            h = g * hp + hh
            return (i, 0, jnp.where(h < na_heads, mha0 + h, gqa0 + (h - na_heads) // group))
        return pl.BlockSpec((None, l, hd), index)

    return pl.pallas_call(
        functools.partial(_ctx_kernel, hp=hp),
        grid=(b, (na_heads + wa_heads) // hp),
        in_specs=[pl.BlockSpec(memory_space=pltpu.SMEM), pl.BlockSpec((None, l, hp * hd), qmap)]
        + [kv_spec(hh, na_heads, kb0) for hh in range(hp)]
        + [kv_spec(hh, 2 * na_heads, vb0) for hh in range(hp)],
        out_specs=pl.BlockSpec((None, l, hp * hd), lambda i, g: (i, 0, g)),
        out_shape=jax.ShapeDtypeStruct((b, l, (na_heads + wa_heads) * hd), BF16),
        compiler_params=_params(2),
        name="context_attn",
    )(sink_all, pc, *([pc] * (2 * hp)))


DFT_RADIX = 4
def _dft_cos_sin(n, ks, nt):
    lo = 1
    while lo * lo < nt:
        lo *= 2
    hi = nt // lo
    k = jnp.asarray(ks, dtype=jnp.int32)
    a = (2.0 * np.pi / n) * ((k[:, None] * (jnp.arange(hi, dtype=jnp.int32) * lo)[None, :]) % n).astype(F32)
    bb = (2.0 * np.pi / n) * ((k[:, None] * jnp.arange(lo, dtype=jnp.int32)[None, :]) % n).astype(F32)
    ca, sa, cb, sb = jnp.cos(a)[:, :, None], jnp.sin(a)[:, :, None], jnp.cos(bb)[:, None, :], jnp.sin(bb)[:, None, :]
    return (ca * cb - sa * sb).reshape(len(ks), nt), (sa * cb + ca * sb).reshape(len(ks), nt)


def _group_dft_kernel(u0_ref, u1_ref, u2_ref, u3_ref, cs_ref, o_ref, *, groups):
    gd = FNET_GROUP_DIM
    cs = cs_ref[...]
    for g in range(groups):
        cols = slice(g * gd, (g + 1) * gd)
        z = [jnp.dot(u_ref[:, cols], cs, preferred_element_type=F32) for u_ref in (u0_ref, u1_ref, u2_ref, u3_ref)]
        (a0, b0), (a1, b1), (a2, b2), (a3, b3) = [(zq[:, :gd], zq[:, gd:]) for zq in z]
        w_parts = [
            (a0 + a1 + a2 + a3, b0 + b1 + b2 + b3),
            (a0 - b1 - a2 + b3, b0 + a1 - b2 - a3),
            (a0 - a1 + a2 - a3, b0 - b1 + b2 - b3),
            (a0 + b1 - a2 - b3, b0 - a1 - b2 + a3),
        ]
        for r, (re, im) in enumerate(w_parts):
            o_ref[r, 0, :, cols] = re.astype(o_ref.dtype)
            o_ref[r, 1, :, cols] = im.astype(o_ref.dtype)


def _group_dft(p, width):
    b, t, _ = p.shape
    gd = FNET_GROUP_DIM
    c, s = _dft_cos_sin(gd, np.arange(gd), gd)
    cs = (jnp.concatenate([c, s], axis=1) * (gd ** -0.5)).astype(BF16)
    tq = t // DFT_RADIX
    tt = _pick(tq, 256)
    nq = tq // tt

    def rows(q):
        return pl.BlockSpec((None, tt, width), lambda i, j: (i, q * nq + j, 0))

    return pl.pallas_call(
        functools.partial(_group_dft_kernel, groups=width // gd),
        grid=(b, nq),
        in_specs=[rows(q) for q in range(DFT_RADIX)] + [pl.BlockSpec((gd, 2 * gd), lambda i, j: (0, 0))],
        out_specs=pl.BlockSpec((None, DFT_RADIX, 2, tt, width), lambda i, j: (i, 0, 0, j, 0)),
        out_shape=jax.ShapeDtypeStruct((b, DFT_RADIX, 2, tq, width), BF16),
        compiler_params=_params(2),
        name="group_dft",
    )(p, p, p, p, cs)


def _seq_dft(v):
    b, radix, _, tq, w = v.shape
    t = radix * tq
    mats = []
    for r in range(radix):
        c, s = _dft_cos_sin(t, np.arange(r, t, radix), tq)
        mats.append(jnp.concatenate([c, -s], axis=1) * (t ** -0.5))
    a = jnp.stack(mats).astype(BF16)
    tm = _pick(tq, 1024)
    tn = _pick(w, 1024)
    nj = w // tn
    out = pl.pallas_call(
        functools.partial(_mm_kernel, n_parts=1),
        grid=(b, radix, nj, tq // tm),
        in_specs=[pl.BlockSpec((None, tm, 2 * tq), lambda g, r, j, i: (r, i, 0)),
                  pl.BlockSpec((None, None, 2 * tq, tn), lambda g, r, j, i: (g, r, 0, j))],
        out_specs=pl.BlockSpec((None, tm, tn), lambda g, r, j, i: (g, i, r * nj + j)),
        out_shape=jax.ShapeDtypeStruct((b, tq, radix * w), BF16),
        compiler_params=_params(4),
        name="seq_dft",
    )(a, v.reshape(b, radix, 2 * tq, w))
    return out.reshape(b, t, w)


def _conv_cols(t, width):
    tc = LANES
    while tc * 2 * t <= CONV_BLOCK_ELEMS and width % (tc * 2) == 0:
        tc *= 2
    return tc


def _dwconv3(z, w_ref):
    t = z.shape[0]
    sub = min(F32_SUBLANES, t)
    row = lax.broadcasted_iota(jnp.int32, (sub, z.shape[1]), 0)
    zp = pltpu.roll(z, 1, axis=0)
    zn = pltpu.roll(z, t - 1, axis=0)
    zp = jnp.concatenate([jnp.where(row == 0, 0.0, zp[:sub]), zp[sub:]], axis=0)
    zn = jnp.concatenate([zn[:t - sub], jnp.where(row == sub - 1, 0.0, zn[t - sub:])], axis=0)
    return zp * w_ref[0:1, :] + z * w_ref[1:2, :] + zn * w_ref[2:3, :]


def _shortconv_kernel(bg_ref, cg_ref, hv_ref, w_ref, o_ref):
    z = cg_ref[...].astype(F32) * hv_ref[...].astype(F32)
    o_ref[...] = (bg_ref[...].astype(F32) * _dwconv3(z, w_ref)).astype(o_ref.dtype)


def _shortconv(p, conv_w, layer, col0, width):
    b, t, _ = p.shape
    tc = _conv_cols(t, math.gcd(width, col0))
    nb = width // tc
    c0 = col0 // tc
    return pl.pallas_call(
        _shortconv_kernel,
        grid=(b, nb),
        in_specs=[pl.BlockSpec((None, t, tc), lambda i, j: (i, 0, c0 + j)),
                  pl.BlockSpec((None, t, tc), lambda i, j: (i, 0, c0 + nb + j)),
                  pl.BlockSpec((None, t, tc), lambda i, j: (i, 0, c0 + 2 * nb + j)),
                  pl.BlockSpec((None, 3, tc), lambda i, j: (layer, 0, j))],
        out_specs=pl.BlockSpec((None, t, tc), lambda i, j: (i, 0, j)),
        out_shape=jax.ShapeDtypeStruct((b, t, width), BF16),
        compiler_params=_params(2),
        name="shortconv_gate",
    )(p, p, p, conv_w)


def _ffn_gate_kernel(g_ref, v_ref, wg_ref, wv_ref, o_ref):
    gate = _dwconv3(g_ref[...].astype(F32), wg_ref)
    val = _dwconv3(v_ref[...].astype(F32), wv_ref)
    o_ref[...] = (gate * val / (1.0 + jnp.exp2(gate * (-LOG2E)))).astype(o_ref.dtype)


def _ffn_gate(u, conv_w, layer):
    b, t, f2 = u.shape
    f = f2 // 2
    tc = _conv_cols(t, f)
    nb = f // tc
    return pl.pallas_call(
        _ffn_gate_kernel,
        grid=(b, nb),
        in_specs=[pl.BlockSpec((None, t, tc), lambda i, j: (i, 0, j)),
                  pl.BlockSpec((None, t, tc), lambda i, j: (i, 0, nb + j)),
                  pl.BlockSpec((None, 3, tc), lambda i, j: (layer, 0, j)),
                  pl.BlockSpec((None, 3, tc), lambda i, j: (layer, 0, nb + j))],
        out_specs=pl.BlockSpec((None, t, tc), lambda i, j: (i, 0, j)),
        out_shape=jax.ShapeDtypeStruct((b, t, f), BF16),
        compiler_params=_params(2),
        name="ffn_conv_gate",
    )(u, u, conv_w, conv_w)


def _proj(parts, w, layer, out_dtype, **kw):
    b, t, _ = parts[0].shape
    y = _matmul([a.reshape(b * t, a.shape[2]) for a in parts], w, layer, out_dtype, **kw)
    return y.reshape(b, t, -1)


def _even_mixer(hl, hc, w_in, rpb, sink, w_out, e, rope, ctx_live):
    na_heads = rpb.shape[1]
    wa_heads = sink.shape[1]
    na_w = na_heads * HEAD_DIM
    wa_w = wa_heads * HEAD_DIM
    kv_w = WA_KV_HEADS * HEAD_DIM
    p = _proj([hl], w_in, e, BF16, name="even_w_in")
    pc = _proj([hc], w_in, e, BF16, name="even_w_in_ctx")
    oa = _neighborhood_attn(p, pc, rpb[e], na_heads)
    ob = _window_gqa(p, pc, sink[e].astype(F32), 3 * na_w, 3 * na_w + wa_w, 3 * na_w + wa_w + kv_w,
                     wa_heads, rope)
    yl = _proj([oa, ob], w_out, e, BF16, name="even_w_out")
    yc = None
    if ctx_live:
        yc = _proj([_ctx_attn(pc, sink[e], na_heads, wa_heads)], w_out, e, BF16, name="even_w_out_ctx")
    return yl, yc


def _odd_mixer(h, w_in, conv_w, w_out, o):
    sc_w = conv_w.shape[2]
    f_w = w_in.shape[2] - 3 * sc_w
    p = _proj([h], w_in, o, BF16, name="odd_w_in")
    f = _seq_dft(_group_dft(p, f_w))
    sc = _shortconv(p, conv_w, o, f_w, sc_w)
    return _proj([f, sc], w_out, o, BF16, name="odd_w_out")


def _conv_ffn(h, w_up, conv_w, w_down, i):
    u = _proj([h], w_up, i, BF16, name="ffn_w_up")
    a = _ffn_gate(u, conv_w, i)
    return _proj([a], w_down, i, BF16, tm=512, tn=1024, name="ffn_w_down")


def kernel(x, c, ctx, c_ctx, w_mod, b_mod, g_mix_pre, g_mix_post, g_ffn_pre, g_ffn_post, even_w_in, even_rpb, even_sink, even_w_out, odd_w_in, odd_conv, odd_w_out, ffn_w_up, ffn_conv, ffn_w_down):
    b, t, d = x.shape
    depth = w_mod.shape[0]
    rows = -(-(b + 1) // 8) * 8
    cond = jnp.zeros((rows, d), F32).at[:b].set(c).at[b].set(c_ctx)
    mod = _modulation(cond, w_mod, b_mod)
    rope = _rope_tables(t)
    xc = ctx
    hl = _prenorm(x, g_mix_pre[0], mod, 0, 0)
    for i in range(depth):
        ctx_live = any(j % 2 == 0 for j in range(i + 1, depth))
        need_hc = (i % 2 == 0) or ctx_live
        hc = _prenorm(xc, g_mix_pre[i], mod, i, 0, fixed_row=b) if need_hc else None
        if i % 2 == 0:
            e = i // 2
            yl, yc = _even_mixer(hl, hc, even_w_in, even_rpb, even_sink, even_w_out, e, rope, ctx_live)
        else:
            o = i // 2
            yl = _odd_mixer(hl, odd_w_in, odd_conv, odd_w_out, o)
            yc = _odd_mixer(hc, odd_w_in, odd_conv, odd_w_out, o) if ctx_live else None
        x, hl = _residual(x, yl, g_mix_post[i], mod, i, 2, nxt=(g_ffn_pre[i], i, 3))
        yf = _conv_ffn(hl, ffn_w_up, ffn_conv, ffn_w_down, i)
        nxt = (g_mix_pre[i + 1], i + 1, 0) if i + 1 < depth else None
        x, hl = _residual(x, yf, g_ffn_post[i], mod, i, 5, nxt=nxt)
        if ctx_live:
            xc, hcf = _residual(xc, yc, g_mix_post[i], mod, i, 2, nxt=(g_ffn_pre[i], i, 3), fixed_row=b)
            yfc = _conv_ffn(hcf, ffn_w_up, ffn_conv, ffn_w_down, i)
            xc, _ = _residual(xc, yfc, g_ffn_post[i], mod, i, 5, fixed_row=b)
    return x
```

```python
import functools
import math

import numpy as np
import jax
import jax.numpy as jnp
from jax import lax
from jax.experimental import pallas as pl
from jax.experimental.pallas import tpu as pltpu

F32 = jnp.float32
BF16 = jnp.bfloat16

GRID_W = 64
HEAD_DIM = 128
NA_KH = 8
NA_KW = 16
WA_KV_HEADS = 4
WA_WINDOW = 128
WA_BLOCK = 128
FNET_GROUP_DIM = 128
ROPE_THETA = 10000.0
RMS_EPS = 1e-6
NEG_INF = -1e30
LOG2E = 1.4426950408889634

V7X_VMEM_BYTES = 64 * 1024 * 1024
VMEM_LIMIT = V7X_VMEM_BYTES - 8 * 1024 * 1024
LANES = 128

NA_ROW_BLOCK = 4
NA_HEADS_PER_STEP = 8
WA_KV_PER_STEP = 4
CTX_HEADS_PER_STEP = 4
CONV_BLOCK_ELEMS = 4096 * 128
F32_SUBLANES = 8
MIN_ROW_TILES_FOR_STAGING = 8


def _params(n_grid):
    return pltpu.CompilerParams(dimension_semantics=("arbitrary",) * n_grid,
                                vmem_limit_bytes=VMEM_LIMIT)


def _pick(n, pref):
    if n <= pref:
        return n
    t = pref
    while n % t:
        t //= 2
    return t


def _mod_kernel(c_ref, w_ref, b_ref, o_ref):
    c = c_ref[...]
    s = (c / (1.0 + jnp.exp(-c))).astype(BF16)
    y = jnp.dot(s, w_ref[...].astype(BF16), preferred_element_type=F32)
    o_ref[...] = y + b_ref[...]


def _modulation(cond, w_mod, b_mod):
    depth, d, n = w_mod.shape
    rows = cond.shape[0]
    tn = _pick(n, 512)
    return pl.pallas_call(
        _mod_kernel,
        grid=(depth, n // tn),
        in_specs=[pl.BlockSpec((rows, d), lambda l, j: (0, 0)),
                  pl.BlockSpec((None, d, tn), lambda l, j: (l, 0, j)),
                  pl.BlockSpec((None, 1, tn), lambda l, j: (l, 0, j))],
        out_specs=pl.BlockSpec((None, rows, tn), lambda l, j: (l, 0, j)),
        out_shape=jax.ShapeDtypeStruct((depth, rows, n), F32),
        compiler_params=_params(2),
        name="adaln_mod",
    )(cond, w_mod, b_mod.reshape(depth, 1, n))


def _rms(v, g):
    return v * lax.rsqrt(jnp.mean(v * v, axis=-1, keepdims=True) + RMS_EPS) * g


def _mod_row(ref, fixed_row):
    row = pl.program_id(0) if fixed_row is None else fixed_row
    return ref[pl.ds(row, 1), :]


def _prenorm_kernel(x_ref, g_ref, sh_ref, sc_ref, h_ref, *, fixed_row):
    sh = _mod_row(sh_ref, fixed_row)
    sc = _mod_row(sc_ref, fixed_row)
    h_ref[...] = (_rms(x_ref[...], g_ref[...]) * (1.0 + sc) + sh).astype(h_ref.dtype)


def _prenorm(x, g, mod, layer, slot, fixed_row=None):
    b, t, d = x.shape
    rows = mod.shape[1]
    tr = _pick(t, 256)
    return pl.pallas_call(
        functools.partial(_prenorm_kernel, fixed_row=fixed_row),
        grid=(b, t // tr),
        in_specs=[pl.BlockSpec((None, tr, d), lambda i, j: (i, j, 0)),
                  pl.BlockSpec((1, d), lambda i, j: (0, 0)),
                  pl.BlockSpec((None, rows, d), lambda i, j: (layer, 0, slot)),
                  pl.BlockSpec((None, rows, d), lambda i, j: (layer, 0, slot + 1))],
        out_specs=pl.BlockSpec((None, tr, d), lambda i, j: (i, j, 0)),
        out_shape=jax.ShapeDtypeStruct((b, t, d), BF16),
        compiler_params=_params(2),
        name="prenorm",
    )(x, g.reshape(1, d), mod, mod)


def _resid_kernel(x_ref, y_ref, gp_ref, gt_ref, *rest, fixed_row, with_next):
    gt = _mod_row(gt_ref, fixed_row)
    xn = x_ref[...] + gt * _rms(y_ref[...].astype(F32), gp_ref[...])
    if with_next:
        gn_ref, sh_ref, sc_ref, xo_ref, h_ref = rest
        sh = _mod_row(sh_ref, fixed_row)
        sc = _mod_row(sc_ref, fixed_row)
        h_ref[...] = (_rms(xn, gn_ref[...]) * (1.0 + sc) + sh).astype(h_ref.dtype)
    else:
        (xo_ref,) = rest
    xo_ref[...] = xn


def _residual(x, y, g_post, mod, layer, gate_slot, nxt=None, fixed_row=None):
    b, t, d = x.shape
    rows = mod.shape[1]
    tr = _pick(t, 256)
    row_spec = pl.BlockSpec((None, tr, d), lambda i, j: (i, j, 0))
    vec_spec = pl.BlockSpec((1, d), lambda i, j: (0, 0))

    def mod_spec(l, s):
        return pl.BlockSpec((None, rows, d), lambda i, j: (l, 0, s))

    in_specs = [row_spec, row_spec, vec_spec, mod_spec(layer, gate_slot)]
    args = [x, y, g_post.reshape(1, d), mod]
    out_specs = [row_spec]
    out_shape = [jax.ShapeDtypeStruct((b, t, d), F32)]
    if nxt is not None:
        g_pre, nl, ns = nxt
        in_specs += [vec_spec, mod_spec(nl, ns), mod_spec(nl, ns + 1)]
        args += [g_pre.reshape(1, d), mod, mod]
        out_specs.append(row_spec)
        out_shape.append(jax.ShapeDtypeStruct((b, t, d), BF16))
    out = pl.pallas_call(
        functools.partial(_resid_kernel, fixed_row=fixed_row, with_next=nxt is not None),
        grid=(b, t // tr),
        in_specs=in_specs, out_specs=out_specs, out_shape=out_shape,
        compiler_params=_params(2),
        name="residual_norm",
    )(*args)
    return (out[0], out[1]) if nxt is not None else (out[0], None)


def _dot_parts(a_refs, w_ref, o_ref):
    acc = None
    k0 = 0
    for a_ref in a_refs:
        kp = a_ref.shape[1]
        part = jnp.dot(a_ref[...], w_ref[k0:k0 + kp, :], preferred_element_type=F32)
        acc = part if acc is None else acc + part
        k0 += kp
    o_ref[...] = acc.astype(o_ref.dtype)


def _mm_kernel(*refs, n_parts):
    _dot_parts(refs[:n_parts], refs[n_parts], refs[n_parts + 1])


def _mm_cast_kernel(*refs, n_parts):
    a_refs = refs[:n_parts]
    w_ref, o_ref, wbf = refs[n_parts:]

    @pl.when(pl.program_id(1) == 0)
    def _():
        wbf[...] = w_ref[...].astype(BF16)

    _dot_parts(a_refs, wbf, o_ref)


def _mm_staged_kernel(*refs, n_parts, layer, tn):
    a_refs = refs[:n_parts]
    w_hbm, o_ref, stage, wbf, sem = refs[n_parts:]
    j = pl.program_id(0)

    def tile_copy(jj):
        return pltpu.make_async_copy(w_hbm.at[layer, :, pl.ds(pl.multiple_of(jj * tn, tn), tn)], stage, sem)

    @pl.when(pl.program_id(1) == 0)
    def _():
        @pl.when(j == 0)
        def _():
            tile_copy(0).start()

        tile_copy(j).wait()
        wbf[...] = stage[...].astype(BF16)

        @pl.when(j + 1 < pl.num_programs(0))
        def _():
            tile_copy(j + 1).start()

    _dot_parts(a_refs, wbf, o_ref)


def _matmul(a_parts, w, layer, out_dtype, *, tm=1024, tn=1024, name="matmul"):
    m = a_parts[0].shape[0]
    _, k, n = w.shape
    assert sum(a.shape[1] for a in a_parts) == k
    tm = _pick(m, tm)
    n_parts = len(a_parts)
    a_specs = [pl.BlockSpec((tm, a.shape[1]), lambda j, i: (i, 0)) for a in a_parts]
    out_shape = jax.ShapeDtypeStruct((m, n), out_dtype)
    if m // tm >= MIN_ROW_TILES_FOR_STAGING:
        tn = _pick(n, tn)
        return pl.pallas_call(
            functools.partial(_mm_staged_kernel, n_parts=n_parts, layer=layer, tn=tn),
            grid=(n // tn, m // tm),
            in_specs=a_specs + [pl.BlockSpec(memory_space=pl.ANY)],
            out_specs=pl.BlockSpec((tm, tn), lambda j, i: (i, j)),
            out_shape=out_shape,
            scratch_shapes=[pltpu.VMEM((k, tn), F32), pltpu.VMEM((k, tn), BF16), pltpu.SemaphoreType.DMA(())],
            compiler_params=_params(2),
            name=name,
        )(*a_parts, w)
    tn = _pick(n, tn // 2)
    return pl.pallas_call(
        functools.partial(_mm_cast_kernel, n_parts=n_parts),
        grid=(n // tn, m // tm),
        in_specs=a_specs + [pl.BlockSpec((None, k, tn), lambda j, i: (layer, 0, j))],
        out_specs=pl.BlockSpec((tm, tn), lambda j, i: (i, j)),
        out_shape=out_shape,
        scratch_shapes=[pltpu.VMEM((k, tn), BF16)],
        compiler_params=_params(2),
        name=name,
    )(*a_parts, w)


def _na_geometry(rows, rb):
    kh = min(NA_KH, rows)
    kr = min(rb + kh, rows)
    nblk = rows // rb
    slab = np.clip(np.arange(nblk) * rb - kh // 2, 0, rows - kr)
    variants, variant_of = [], []
    for j in range(nblk):
        geo = []
        for a in range(rb):
            r = j * rb + a
            rs = int(np.clip(r - kh // 2, 0, rows - kh))
            lo = rs - int(slab[j])
            geo.append((lo, lo + kh, rs - r + NA_KH - 1))
        geo = tuple(geo)
        if geo not in variants:
            variants.append(geo)
        variant_of.append(variants.index(geo))
    return kr, slab, np.asarray(variant_of), variants


def _na_bias_table(rpb):
    h = rpb.shape[0]
    qcol = np.arange(GRID_W)
    cstart = np.clip(qcol - NA_KW // 2, 0, GRID_W - NA_KW)
    kcol = np.arange(GRID_W)
    col_ok = (kcol[None, :] >= cstart[:, None]) & (kcol[None, :] < cstart[:, None] + NA_KW)
    ext = GRID_W - NA_KW
    padded = jnp.pad(rpb.astype(F32), ((0, 0), (0, 0), (ext, ext)))
    e = jnp.stack([padded[:, :, GRID_W - 1 - qc:2 * GRID_W - 1 - qc] for qc in range(GRID_W)], axis=1)
    e = jnp.where(jnp.asarray(col_ok)[None, :, None, :], e, NEG_INF)
    e = e.reshape(h, GRID_W, (2 * NA_KH - 1) * GRID_W)
    return jnp.pad(e, ((0, 0), (0, 0), (0, GRID_W)), constant_values=NEG_INF)


def _nt_dot(a, b):
    return lax.dot_general(a, b, (((1,), (1,)), ((), ())), preferred_element_type=F32)


def _na_kernel(slab_ref, var_ref, q_ref, k_ref, v_ref, kc_ref, vc_ref, e_ref, o_ref, bias_ref,
               *, kr, hp, variants):
    j = pl.program_id(2)
    gw = GRID_W
    ks = pl.multiple_of(slab_ref[j] * gw, gw)
    nk = kr * gw
    hd = HEAD_DIM
    scale = hd ** -0.5

    changed = (j == 0) | (var_ref[j] != var_ref[jnp.maximum(j - 1, 0)])
    for vi, geo in enumerate(variants):
        @pl.when(changed & (var_ref[j] == vi))
        def _(geo=geo):
            for hh in range(hp):
                for a, (lo, hi, rr0) in enumerate(geo):
                    qrows = slice(a * gw, (a + 1) * gw)
                    if lo > 0:
                        bias_ref[hh, qrows, 0:lo * gw] = jnp.full((gw, lo * gw), NEG_INF, F32)
                    bias_ref[hh, qrows, lo * gw:hi * gw] = e_ref[hh, :, rr0 * gw:(rr0 + hi - lo) * gw]
                    if hi < kr:
                        bias_ref[hh, qrows, hi * gw:] = jnp.full((gw, (kr - hi) * gw), NEG_INF, F32)

    for hh in range(hp):
        cols = slice(hh * hd, (hh + 1) * hd)
        q = q_ref[:, cols]
        s_loc = _nt_dot(q, k_ref[pl.ds(ks, nk), cols]) * scale + bias_ref[hh]
        s_ctx = _nt_dot(q, kc_ref[:, cols]) * scale
        m = jnp.maximum(jnp.max(s_loc, axis=-1, keepdims=True), jnp.max(s_ctx, axis=-1, keepdims=True))
        p_loc = jnp.exp(s_loc - m)
        p_ctx = jnp.exp(s_ctx - m)
        v_ext = jnp.concatenate([v_ref[pl.ds(ks, nk), cols], jnp.ones((nk, hd), BF16)], axis=1)
        vc_ext = jnp.concatenate([vc_ref[:, cols], jnp.ones((vc_ref.shape[0], hd), BF16)], axis=1)
        oe = (jnp.dot(p_loc.astype(BF16), v_ext, preferred_element_type=F32)
              + jnp.dot(p_ctx.astype(BF16), vc_ext, preferred_element_type=F32))
        o_ref[:, cols] = (oe[:, :hd] / oe[:, hd:hd + 1]).astype(o_ref.dtype)


def _neighborhood_attn(p, pc, rpb, n_heads):
    b, t, _ = p.shape
    l = pc.shape[1]
    rows = t // GRID_W
    rb = min(NA_ROW_BLOCK, rows)
    hp = min(NA_HEADS_PER_STEP, n_heads)
    assert n_heads % hp == 0
    ng = n_heads // hp
    kr, slab, variant_of, variants = _na_geometry(rows, rb)
    table = _na_bias_table(rpb)
    nblk = rows // rb
    tq = rb * GRID_W
    w = hp * HEAD_DIM
    grid_spec = pltpu.PrefetchScalarGridSpec(
        num_scalar_prefetch=2,
        grid=(b, ng, nblk),
        in_specs=[
            pl.BlockSpec((None, tq, w), lambda i, h, j, s, v: (i, j, h)),
            pl.BlockSpec((None, t, w), lambda i, h, j, s, v: (i, 0, ng + h)),
            pl.BlockSpec((None, t, w), lambda i, h, j, s, v: (i, 0, 2 * ng + h)),
            pl.BlockSpec((None, l, w), lambda i, h, j, s, v: (i, 0, ng + h)),
            pl.BlockSpec((None, l, w), lambda i, h, j, s, v: (i, 0, 2 * ng + h)),
            pl.BlockSpec((hp, GRID_W, table.shape[2]), lambda i, h, j, s, v: (h, 0, 0)),
        ],
        out_specs=pl.BlockSpec((None, tq, w), lambda i, h, j, s, v: (i, j, h)),
        scratch_shapes=[pltpu.VMEM((hp, tq, kr * GRID_W), F32)],
    )
    return pl.pallas_call(
        functools.partial(_na_kernel, kr=kr, hp=hp, variants=tuple(variants)),
        grid_spec=grid_spec,
        out_shape=jax.ShapeDtypeStruct((b, t, n_heads * HEAD_DIM), BF16),
        compiler_params=_params(3),
        name="neighborhood_attn",
    )(jnp.asarray(slab, jnp.int32), jnp.asarray(variant_of, jnp.int32), p, p, p, pc, pc, table)


def _rope_tables(s):
    tpos = jnp.arange(s)
    row = (tpos // GRID_W).astype(F32)
    col = (tpos % GRID_W).astype(F32)
    nf = HEAD_DIM // 4
    inv = ROPE_THETA ** (-jnp.arange(nf, dtype=F32) / nf)
    lane = np.arange(HEAD_DIM)
    axis = lane // (2 * nf)
    second = (lane % (2 * nf)) >= nf
    freq = lane % nf
    pos = jnp.where(jnp.asarray(axis)[None, :] == 0, row[:, None], col[:, None])
    ang = pos * inv[jnp.asarray(freq)][None, :]
    cos, sin = jnp.cos(ang), jnp.sin(ang)
    sec = jnp.asarray(second)[None, :]
    return cos, jnp.where(sec, 0.0, -sin), jnp.where(sec, sin, 0.0)


def _rope(x, c, s1, s2):
    nf = HEAD_DIM // 4
    return x * c + pltpu.roll(x, HEAD_DIM - nf, axis=1) * s1 + pltpu.roll(x, nf, axis=1) * s2


def _wa_kernel(sink_ref, q_ref, k_ref, v_ref, kc_ref, vc_ref, c_ref, s1_ref, s2_ref, o_ref,
               *, group, kp, seq):
    kg = pl.program_id(1)
    n = pl.program_id(2)
    wb, hd = WA_BLOCK, HEAD_DIM
    nl = 3 * wb
    q0 = pl.multiple_of(n * wb, wb)
    bs = pl.multiple_of(jnp.clip(n * wb - wb, 0, seq - nl), wb)
    scale = hd ** -0.5
    cq, s1q, s2q = c_ref[pl.ds(q0, wb), :], s1_ref[pl.ds(q0, wb), :], s2_ref[pl.ds(q0, wb), :]
    ck, s1k, s2k = c_ref[pl.ds(bs, nl), :], s1_ref[pl.ds(bs, nl), :], s2_ref[pl.ds(bs, nl), :]
    rowi = lax.broadcasted_iota(jnp.int32, (group * wb, nl), 0)
    kpos = bs + lax.broadcasted_iota(jnp.int32, (group * wb, nl), 1)
    qpos = q0 + (rowi & (wb - 1))
    in_window = jnp.abs(kpos - qpos) <= WA_WINDOW
    gi = lax.broadcasted_iota(jnp.int32, (group * wb, 1), 0) // wb
    for kk in range(kp):
        kcols = slice(kk * hd, (kk + 1) * hd)
        q = jnp.concatenate(
            [_rope(q_ref[:, (kk * group + g) * hd:(kk * group + g + 1) * hd].astype(F32), cq, s1q, s2q)
             .astype(BF16) for g in range(group)], axis=0)
        kb = _rope(k_ref[pl.ds(bs, nl), kcols].astype(F32), ck, s1k, s2k).astype(BF16)
        s_loc = jnp.where(in_window, _nt_dot(q, kb) * scale, NEG_INF)
        s_ctx = _nt_dot(q, kc_ref[:, kcols]) * scale
        sink = jnp.zeros((group * wb, 1), F32)
        for g in range(group):
            sink = jnp.where(gi == g, sink_ref[(kg * kp + kk) * group + g], sink)
        m = jnp.maximum(jnp.maximum(jnp.max(s_loc, axis=-1, keepdims=True),
                                    jnp.max(s_ctx, axis=-1, keepdims=True)), sink)
        p_loc = jnp.exp(s_loc - m)
        p_ctx = jnp.exp(s_ctx - m)
        den = (jnp.sum(p_loc, axis=-1, keepdims=True) + jnp.sum(p_ctx, axis=-1, keepdims=True)
               + jnp.exp(sink - m))
        o = (jnp.dot(p_loc.astype(BF16), v_ref[pl.ds(bs, nl), kcols], preferred_element_type=F32)
             + jnp.dot(p_ctx.astype(BF16), vc_ref[:, kcols], preferred_element_type=F32))
        o = (o / den).astype(o_ref.dtype)
        for g in range(group):
            o_ref[:, (kk * group + g) * hd:(kk * group + g + 1) * hd] = o[g * wb:(g + 1) * wb, :]


def _window_gqa(p, pc, sink, q_col, k_col, v_col, n_heads, rope):
    b, t, _ = p.shape
    l = pc.shape[1]
    kvh = WA_KV_HEADS
    group = n_heads // kvh
    kp = min(WA_KV_PER_STEP, kvh)
    hd, wb = HEAD_DIM, WA_BLOCK
    qw = kp * group * hd
    kw = kp * hd
    assert kvh % kp == 0 and q_col % qw == 0 and k_col % kw == 0 and v_col % kw == 0
    assert t % wb == 0 and t >= 3 * wb
    cos, s1, s2 = rope
    tab_spec = pl.BlockSpec((t, hd), lambda i, k, n: (0, 0))
    return pl.pallas_call(
        functools.partial(_wa_kernel, group=group, kp=kp, seq=t),
        grid=(b, kvh // kp, t // wb),
        in_specs=[
            pl.BlockSpec(memory_space=pltpu.SMEM),
            pl.BlockSpec((None, wb, qw), lambda i, k, n: (i, n, q_col // qw + k)),
            pl.BlockSpec((None, t, kw), lambda i, k, n: (i, 0, k_col // kw + k)),
            pl.BlockSpec((None, t, kw), lambda i, k, n: (i, 0, v_col // kw + k)),
            pl.BlockSpec((None, l, kw), lambda i, k, n: (i, 0, k_col // kw + k)),
            pl.BlockSpec((None, l, kw), lambda i, k, n: (i, 0, v_col // kw + k)),
            tab_spec, tab_spec, tab_spec,
        ],
        out_specs=pl.BlockSpec((None, wb, qw), lambda i, k, n: (i, n, k)),
        out_shape=jax.ShapeDtypeStruct((b, t, n_heads * hd), BF16),
        compiler_params=_params(3),
        name="window_gqa",
    )(sink, p, p, p, pc, pc, cos, s1, s2)


def _ctx_kernel(sink_ref, q_ref, *refs, hp):
    k_refs, v_refs, o_ref = refs[:hp], refs[hp:2 * hp], refs[2 * hp]
    hd = HEAD_DIM
    scale = hd ** -0.5
    for hh in range(hp):
        cols = slice(hh * hd, (hh + 1) * hd)
        s = _nt_dot(q_ref[:, cols], k_refs[hh][...]) * scale
        sink = sink_ref[pl.program_id(1) * hp + hh]
        m = jnp.maximum(jnp.max(s, axis=-1, keepdims=True), sink)
        p = jnp.exp(s - m)
        den = jnp.sum(p, axis=-1, keepdims=True) + jnp.exp(sink - m)
        o = jnp.dot(p.astype(BF16), v_refs[hh][...], preferred_element_type=F32)
        o_ref[:, cols] = (o / den).astype(o_ref.dtype)


def _ctx_attn(pc, sink, na_heads, wa_heads):
    b, l, _ = pc.shape
    hd = HEAD_DIM
    group = wa_heads // WA_KV_HEADS
    hp = CTX_HEADS_PER_STEP
    while na_heads % hp or wa_heads % hp:
        hp //= 2
    qb0 = 3 * na_heads
    kb0 = qb0 + wa_heads
    vb0 = kb0 + WA_KV_HEADS
    sink_all = jnp.concatenate([jnp.full((na_heads,), NEG_INF, F32), sink.astype(F32)])

    def qmap(i, g):
        return (i, 0, jnp.where(g * hp < na_heads, g, (qb0 - na_heads) // hp + g))

    def kv_spec(hh, mha0, gqa0):
        def index(i, g):
            h = g * hp + hh
            return (i, 0, jnp.where(h < na_heads, mha0 + h, gqa0 + (h - na_heads) // group))
        return pl.BlockSpec((None, l, hd), index)

    return pl.pallas_call(
        functools.partial(_ctx_kernel, hp=hp),
        grid=(b, (na_heads + wa_heads) // hp),
        in_specs=[pl.BlockSpec(memory_space=pltpu.SMEM), pl.BlockSpec((None, l, hp * hd), qmap)]
        + [kv_spec(hh, na_heads, kb0) for hh in range(hp)]
        + [kv_spec(hh, 2 * na_heads, vb0) for hh in range(hp)],
        out_specs=pl.BlockSpec((None, l, hp * hd), lambda i, g: (i, 0, g)),
        out_shape=jax.ShapeDtypeStruct((b, l, (na_heads + wa_heads) * hd), BF16),
        compiler_params=_params(2),
        name="context_attn",
    )(sink_all, pc, *([pc] * (2 * hp)))


DFT_RADIX = 4
def _dft_cos_sin(n, ks, nt):
    lo = 1
    while lo * lo < nt:
        lo *= 2
    hi = nt // lo
    k = jnp.asarray(ks, dtype=jnp.int32)
    a = (2.0 * np.pi / n) * ((k[:, None] * (jnp.arange(hi, dtype=jnp.int32) * lo)[None, :]) % n).astype(F32)
    bb = (2.0 * np.pi / n) * ((k[:, None] * jnp.arange(lo, dtype=jnp.int32)[None, :]) % n).astype(F32)
    ca, sa, cb, sb = jnp.cos(a)[:, :, None], jnp.sin(a)[:, :, None], jnp.cos(bb)[:, None, :], jnp.sin(bb)[:, None, :]
    return (ca * cb - sa * sb).reshape(len(ks), nt), (sa * cb + ca * sb).reshape(len(ks), nt)


def _group_dft_kernel(u0_ref, u1_ref, u2_ref, u3_ref, cs_ref, o_ref, *, groups):
    gd = FNET_GROUP_DIM
    cs = cs_ref[...]
    for g in range(groups):
        cols = slice(g * gd, (g + 1) * gd)
        z = [jnp.dot(u_ref[:, cols], cs, preferred_element_type=F32) for u_ref in (u0_ref, u1_ref, u2_ref, u3_ref)]
        (a0, b0), (a1, b1), (a2, b2), (a3, b3) = [(zq[:, :gd], zq[:, gd:]) for zq in z]
        w_parts = [
            (a0 + a1 + a2 + a3, b0 + b1 + b2 + b3),
            (a0 - b1 - a2 + b3, b0 + a1 - b2 - a3),
            (a0 - a1 + a2 - a3, b0 - b1 + b2 - b3),
            (a0 + b1 - a2 - b3, b0 - a1 - b2 + a3),
        ]
        for r, (re, im) in enumerate(w_parts):
            o_ref[r, 0, :, cols] = re.astype(o_ref.dtype)
            o_ref[r, 1, :, cols] = im.astype(o_ref.dtype)


def _group_dft(p, width):
    b, t, _ = p.shape
    gd = FNET_GROUP_DIM
    c, s = _dft_cos_sin(gd, np.arange(gd), gd)
    cs = (jnp.concatenate([c, s], axis=1) * (gd ** -0.5)).astype(BF16)
    tq = t // DFT_RADIX
    tt = _pick(tq, 256)
    nq = tq // tt

    def rows(q):
        return pl.BlockSpec((None, tt, width), lambda i, j: (i, q * nq + j, 0))

    return pl.pallas_call(
        functools.partial(_group_dft_kernel, groups=width // gd),
        grid=(b, nq),
        in_specs=[rows(q) for q in range(DFT_RADIX)] + [pl.BlockSpec((gd, 2 * gd), lambda i, j: (0, 0))],
        out_specs=pl.BlockSpec((None, DFT_RADIX, 2, tt, width), lambda i, j: (i, 0, 0, j, 0)),
        out_shape=jax.ShapeDtypeStruct((b, DFT_RADIX, 2, tq, width), BF16),
        compiler_params=_params(2),
        name="group_dft",
    )(p, p, p, p, cs)


def _seq_dft_kernel(a_ref, v_ref, perm_ref, o_ref, *, radix, sub):
    tm = a_ref.shape[1]
    ys = [jnp.dot(a_ref[r], v_ref[r], preferred_element_type=F32).astype(BF16) for r in range(radix)]
    perm = perm_ref[...]
    for g in range(tm // sub):
        stacked = jnp.concatenate([y[g * sub:(g + 1) * sub, :] for y in ys], axis=0)
        o_ref[g * radix * sub:(g + 1) * radix * sub, :] = jnp.dot(
            perm, stacked, preferred_element_type=F32).astype(o_ref.dtype)


def _seq_dft(v):
    b, radix, _, tq, w = v.shape
    t = radix * tq
    mats = []
    for r in range(radix):
        c, s = _dft_cos_sin(t, np.arange(r, t, radix), tq)
        mats.append(jnp.concatenate([c, -s], axis=1) * (t ** -0.5))
    a = jnp.stack(mats).astype(BF16)
    tm = _pick(tq, 512)
    tn = _pick(w, 512)
    sub = min(tm, 2 * LANES // radix)
    rows = np.arange(radix * sub)
    perm = np.zeros((radix * sub, radix * sub), np.float32)
    perm[rows, (rows % radix) * sub + rows // radix] = 1.0
    return pl.pallas_call(
        functools.partial(_seq_dft_kernel, radix=radix, sub=sub),
        grid=(b, w // tn, tq // tm),
        in_specs=[pl.BlockSpec((radix, tm, 2 * tq), lambda g, j, i: (0, i, 0)),
                  pl.BlockSpec((None, radix, 2 * tq, tn), lambda g, j, i: (g, 0, 0, j)),
                  pl.BlockSpec((radix * sub, radix * sub), lambda g, j, i: (0, 0))],
        out_specs=pl.BlockSpec((None, radix * tm, tn), lambda g, j, i: (g, i, j)),
        out_shape=jax.ShapeDtypeStruct((b, t, w), BF16),
        compiler_params=_params(3),
        name="seq_dft",
    )(a, v.reshape(b, radix, 2 * tq, w), jnp.asarray(perm, BF16))


def _conv_cols(t, width):
    tc = LANES
    while tc * 2 * t <= CONV_BLOCK_ELEMS and width % (tc * 2) == 0:
        tc *= 2
    return tc


def _dwconv3(z, w_ref):
    t = z.shape[0]
    sub = min(F32_SUBLANES, t)
    row = lax.broadcasted_iota(jnp.int32, (sub, z.shape[1]), 0)
    zp = pltpu.roll(z, 1, axis=0)
    zn = pltpu.roll(z, t - 1, axis=0)
    zp = jnp.concatenate([jnp.where(row == 0, 0.0, zp[:sub]), zp[sub:]], axis=0)
    zn = jnp.concatenate([zn[:t - sub], jnp.where(row == sub - 1, 0.0, zn[t - sub:])], axis=0)
    return zp * w_ref[0:1, :] + z * w_ref[1:2, :] + zn * w_ref[2:3, :]


def _shortconv_kernel(bg_ref, cg_ref, hv_ref, w_ref, o_ref):
    z = cg_ref[...].astype(F32) * hv_ref[...].astype(F32)
    o_ref[...] = (bg_ref[...].astype(F32) * _dwconv3(z, w_ref)).astype(o_ref.dtype)


def _shortconv(p, conv_w, layer, col0, width):
    b, t, _ = p.shape
    tc = _conv_cols(t, math.gcd(width, col0))
    nb = width // tc
    c0 = col0 // tc
    return pl.pallas_call(
        _shortconv_kernel,
        grid=(b, nb),
        in_specs=[pl.BlockSpec((None, t, tc), lambda i, j: (i, 0, c0 + j)),
                  pl.BlockSpec((None, t, tc), lambda i, j: (i, 0, c0 + nb + j)),
                  pl.BlockSpec((None, t, tc), lambda i, j: (i, 0, c0 + 2 * nb + j)),
                  pl.BlockSpec((None, 3, tc), lambda i, j: (layer, 0, j))],
        out_specs=pl.BlockSpec((None, t, tc), lambda i, j: (i, 0, j)),
        out_shape=jax.ShapeDtypeStruct((b, t, width), BF16),
        compiler_params=_params(2),
        name="shortconv_gate",
    )(p, p, p, conv_w)


def _ffn_gate_kernel(g_ref, v_ref, wg_ref, wv_ref, o_ref):
    gate = _dwconv3(g_ref[...].astype(F32), wg_ref)
    val = _dwconv3(v_ref[...].astype(F32), wv_ref)
    o_ref[...] = (gate * val / (1.0 + jnp.exp2(gate * (-LOG2E)))).astype(o_ref.dtype)


def _ffn_gate(u, conv_w, layer):
    b, t, f2 = u.shape
    f = f2 // 2
    tc = _conv_cols(t, f)
    nb = f // tc
    return pl.pallas_call(
        _ffn_gate_kernel,
        grid=(b, nb),
        in_specs=[pl.BlockSpec((None, t, tc), lambda i, j: (i, 0, j)),
                  pl.BlockSpec((None, t, tc), lambda i, j: (i, 0, nb + j)),
                  pl.BlockSpec((None, 3, tc), lambda i, j: (layer, 0, j)),
                  pl.BlockSpec((None, 3, tc), lambda i, j: (layer, 0, nb + j))],
        out_specs=pl.BlockSpec((None, t, tc), lambda i, j: (i, 0, j)),
        out_shape=jax.ShapeDtypeStruct((b, t, f), BF16),
        compiler_params=_params(2),
        name="ffn_conv_gate",
    )(u, u, conv_w, conv_w)


def _proj(parts, w, layer, out_dtype, **kw):
    b, t, _ = parts[0].shape
    y = _matmul([a.reshape(b * t, a.shape[2]) for a in parts], w, layer, out_dtype, **kw)
    return y.reshape(b, t, -1)


def _even_mixer(hl, hc, w_in, rpb, sink, w_out, e, rope, ctx_live):
    na_heads = rpb.shape[1]
    wa_heads = sink.shape[1]
    na_w = na_heads * HEAD_DIM
    wa_w = wa_heads * HEAD_DIM
    kv_w = WA_KV_HEADS * HEAD_DIM
    p = _proj([hl], w_in, e, BF16, name="even_w_in")
    pc = _proj([hc], w_in, e, BF16, name="even_w_in_ctx")
    oa = _neighborhood_attn(p, pc, rpb[e], na_heads)
    ob = _window_gqa(p, pc, sink[e].astype(F32), 3 * na_w, 3 * na_w + wa_w, 3 * na_w + wa_w + kv_w,
                     wa_heads, rope)
    yl = _proj([oa, ob], w_out, e, BF16, name="even_w_out")
    yc = None
    if ctx_live:
        yc = _proj([_ctx_attn(pc, sink[e], na_heads, wa_heads)], w_out, e, BF16, name="even_w_out_ctx")
    return yl, yc


def _odd_mixer(h, w_in, conv_w, w_out, o):
    sc_w = conv_w.shape[2]
    f_w = w_in.shape[2] - 3 * sc_w
    p = _proj([h], w_in, o, BF16, name="odd_w_in")
    f = _seq_dft(_group_dft(p, f_w))
    sc = _shortconv(p, conv_w, o, f_w, sc_w)
    return _proj([f, sc], w_out, o, BF16, name="odd_w_out")


def _conv_ffn(h, w_up, conv_w, w_down, i):
    u = _proj([h], w_up, i, BF16, name="ffn_w_up")
    a = _ffn_gate(u, conv_w, i)
    return _proj([a], w_down, i, BF16, tm=512, tn=1024, name="ffn_w_down")


def kernel(x, c, ctx, c_ctx, w_mod, b_mod, g_mix_pre, g_mix_post, g_ffn_pre, g_ffn_post, even_w_in, even_rpb, even_sink, even_w_out, odd_w_in, odd_conv, odd_w_out, ffn_w_up, ffn_conv, ffn_w_down):
    b, t, d = x.shape
    depth = w_mod.shape[0]
    rows = -(-(b + 1) // 8) * 8
    cond = jnp.zeros((rows, d), F32).at[:b].set(c).at[b].set(c_ctx)
    mod = _modulation(cond, w_mod, b_mod)
    rope = _rope_tables(t)
    xc = ctx
    hl = _prenorm(x, g_mix_pre[0], mod, 0, 0)
    for i in range(depth):
        ctx_live = any(j % 2 == 0 for j in range(i + 1, depth))
        need_hc = (i % 2 == 0) or ctx_live
        hc = _prenorm(xc, g_mix_pre[i], mod, i, 0, fixed_row=b) if need_hc else None
        if i % 2 == 0:
            e = i // 2
            yl, yc = _even_mixer(hl, hc, even_w_in, even_rpb, even_sink, even_w_out, e, rope, ctx_live)
        else:
            o = i // 2
            yl = _odd_mixer(hl, odd_w_in, odd_conv, odd_w_out, o)
            yc = _odd_mixer(hc, odd_w_in, odd_conv, odd_w_out, o) if ctx_live else None
        x, hl = _residual(x, yl, g_mix_post[i], mod, i, 2, nxt=(g_ffn_pre[i], i, 3))
        yf = _conv_ffn(hl, ffn_w_up, ffn_conv, ffn_w_down, i)
        nxt = (g_mix_pre[i + 1], i + 1, 0) if i + 1 < depth else None
        x, hl = _residual(x, yf, g_ffn_post[i], mod, i, 5, nxt=nxt)
        if ctx_live:
            xc, hcf = _residual(xc, yc, g_mix_post[i], mod, i, 2, nxt=(g_ffn_pre[i], i, 3), fixed_row=b)
            yfc = _conv_ffn(hcf, ffn_w_up, ffn_conv, ffn_w_down, i)
            xc, _ = _residual(xc, yfc, g_ffn_post[i], mod, i, 5, fixed_row=b)
    return x
```

```python
import functools
import math

import numpy as np
import jax
import jax.numpy as jnp
from jax import lax
from jax.experimental import pallas as pl
from jax.experimental.pallas import tpu as pltpu

F32 = jnp.float32
BF16 = jnp.bfloat16

GRID_W = 64
HEAD_DIM = 128
NA_KH = 8
NA_KW = 16
WA_KV_HEADS = 4
WA_WINDOW = 128
WA_BLOCK = 128
FNET_GROUP_DIM = 128
ROPE_THETA = 10000.0
RMS_EPS = 1e-6
NEG_INF = -1e30
LOG2E = 1.4426950408889634

V7X_VMEM_BYTES = 64 * 1024 * 1024
VMEM_LIMIT = V7X_VMEM_BYTES - 8 * 1024 * 1024
LANES = 128

NA_ROW_BLOCK = 4
NA_HEADS_PER_STEP = 8
WA_KV_PER_STEP = 4
CTX_HEADS_PER_STEP = 4
CONV_BLOCK_ELEMS = 4096 * 128
F32_SUBLANES = 8


def _params(n_grid):
    return pltpu.CompilerParams(dimension_semantics=("arbitrary",) * n_grid,
                                vmem_limit_bytes=VMEM_LIMIT)


def _pick(n, pref):
    if n <= pref:
        return n
    t = pref
    while n % t:
        t //= 2
    return t


def _mod_kernel(c_ref, w_ref, b_ref, o_ref):
    c = c_ref[...]
    s = (c / (1.0 + jnp.exp(-c))).astype(BF16)
    y = jnp.dot(s, w_ref[...].astype(BF16), preferred_element_type=F32)
    o_ref[...] = y + b_ref[...]


def _modulation(cond, w_mod, b_mod):
    depth, d, n = w_mod.shape
    rows = cond.shape[0]
    tn = _pick(n, 512)
    return pl.pallas_call(
        _mod_kernel,
        grid=(depth, n // tn),
        in_specs=[pl.BlockSpec((rows, d), lambda l, j: (0, 0)),
                  pl.BlockSpec((None, d, tn), lambda l, j: (l, 0, j)),
                  pl.BlockSpec((None, 1, tn), lambda l, j: (l, 0, j))],
        out_specs=pl.BlockSpec((None, rows, tn), lambda l, j: (l, 0, j)),
        out_shape=jax.ShapeDtypeStruct((depth, rows, n), F32),
        compiler_params=_params(2),
        name="adaln_mod",
    )(cond, w_mod, b_mod.reshape(depth, 1, n))


def _rms(v, g):
    return v * lax.rsqrt(jnp.mean(v * v, axis=-1, keepdims=True) + RMS_EPS) * g


def _mod_row(ref, fixed_row):
    row = pl.program_id(0) if fixed_row is None else fixed_row
    return ref[pl.ds(row, 1), :]


def _prenorm_kernel(x_ref, g_ref, sh_ref, sc_ref, h_ref, *, fixed_row):
    sh = _mod_row(sh_ref, fixed_row)
    sc = _mod_row(sc_ref, fixed_row)
    h_ref[...] = (_rms(x_ref[...], g_ref[...]) * (1.0 + sc) + sh).astype(h_ref.dtype)


def _prenorm(x, g, mod, layer, slot, fixed_row=None):
    b, t, d = x.shape
    rows = mod.shape[1]
    tr = _pick(t, 256)
    return pl.pallas_call(
        functools.partial(_prenorm_kernel, fixed_row=fixed_row),
        grid=(b, t // tr),
        in_specs=[pl.BlockSpec((None, tr, d), lambda i, j: (i, j, 0)),
                  pl.BlockSpec((1, d), lambda i, j: (0, 0)),
                  pl.BlockSpec((None, rows, d), lambda i, j: (layer, 0, slot)),
                  pl.BlockSpec((None, rows, d), lambda i, j: (layer, 0, slot + 1))],
        out_specs=pl.BlockSpec((None, tr, d), lambda i, j: (i, j, 0)),
        out_shape=jax.ShapeDtypeStruct((b, t, d), BF16),
        compiler_params=_params(2),
        name="prenorm",
    )(x, g.reshape(1, d), mod, mod)


def _resid_kernel(x_ref, y_ref, gp_ref, gt_ref, *rest, fixed_row, with_next):
    gt = _mod_row(gt_ref, fixed_row)
    xn = x_ref[...] + gt * _rms(y_ref[...].astype(F32), gp_ref[...])
    if with_next:
        gn_ref, sh_ref, sc_ref, xo_ref, h_ref = rest
        sh = _mod_row(sh_ref, fixed_row)
        sc = _mod_row(sc_ref, fixed_row)
        h_ref[...] = (_rms(xn, gn_ref[...]) * (1.0 + sc) + sh).astype(h_ref.dtype)
    else:
        (xo_ref,) = rest
    xo_ref[...] = xn


def _residual(x, y, g_post, mod, layer, gate_slot, nxt=None, fixed_row=None):
    b, t, d = x.shape
    rows = mod.shape[1]
    tr = _pick(t, 256)
    row_spec = pl.BlockSpec((None, tr, d), lambda i, j: (i, j, 0))
    vec_spec = pl.BlockSpec((1, d), lambda i, j: (0, 0))

    def mod_spec(l, s):
        return pl.BlockSpec((None, rows, d), lambda i, j: (l, 0, s))

    in_specs = [row_spec, row_spec, vec_spec, mod_spec(layer, gate_slot)]
    args = [x, y, g_post.reshape(1, d), mod]
    out_specs = [row_spec]
    out_shape = [jax.ShapeDtypeStruct((b, t, d), F32)]
    if nxt is not None:
        g_pre, nl, ns = nxt
        in_specs += [vec_spec, mod_spec(nl, ns), mod_spec(nl, ns + 1)]
        args += [g_pre.reshape(1, d), mod, mod]
        out_specs.append(row_spec)
        out_shape.append(jax.ShapeDtypeStruct((b, t, d), BF16))
    out = pl.pallas_call(
        functools.partial(_resid_kernel, fixed_row=fixed_row, with_next=nxt is not None),
        grid=(b, t // tr),
        in_specs=in_specs, out_specs=out_specs, out_shape=out_shape,
        compiler_params=_params(2),
        name="residual_norm",
    )(*args)
    return (out[0], out[1]) if nxt is not None else (out[0], None)


def _dot_parts(a_refs, w_ref, o_ref):
    acc = None
    k0 = 0
    for a_ref in a_refs:
        kp = a_ref.shape[1]
        part = jnp.dot(a_ref[...], w_ref[k0:k0 + kp, :], preferred_element_type=F32)
        acc = part if acc is None else acc + part
        k0 += kp
    o_ref[...] = acc.astype(o_ref.dtype)


def _mm_kernel(*refs, n_parts):
    _dot_parts(refs[:n_parts], refs[n_parts], refs[n_parts + 1])


def _mm_staged_kernel(*refs, n_parts, layer, tn):
    a_refs = refs[:n_parts]
    w_hbm, o_ref, stage, wbf, sem = refs[n_parts:]
    j = pl.program_id(0)

    def tile_copy(jj):
        return pltpu.make_async_copy(w_hbm.at[layer, :, pl.ds(pl.multiple_of(jj * tn, tn), tn)], stage, sem)

    @pl.when(pl.program_id(1) == 0)
    def _():
        @pl.when(j == 0)
        def _():
            tile_copy(0).start()

        tile_copy(j).wait()
        wbf[...] = stage[...].astype(BF16)

        @pl.when(j + 1 < pl.num_programs(0))
        def _():
            tile_copy(j + 1).start()

    _dot_parts(a_refs, wbf, o_ref)


def _matmul(a_parts, w, layer, out_dtype, *, tm=1024, tn=1024, name="matmul"):
    m = a_parts[0].shape[0]
    _, k, n = w.shape
    assert sum(a.shape[1] for a in a_parts) == k
    tm = _pick(m, tm)
    n_parts = len(a_parts)
    a_specs = [pl.BlockSpec((tm, a.shape[1]), lambda j, i: (i, 0)) for a in a_parts]
    tn = _pick(n, tn)
    return pl.pallas_call(
        functools.partial(_mm_staged_kernel, n_parts=n_parts, layer=layer, tn=tn),
        grid=(n // tn, m // tm),
        in_specs=a_specs + [pl.BlockSpec(memory_space=pl.ANY)],
        out_specs=pl.BlockSpec((tm, tn), lambda j, i: (i, j)),
        out_shape=jax.ShapeDtypeStruct((m, n), out_dtype),
        scratch_shapes=[pltpu.VMEM((k, tn), F32), pltpu.VMEM((k, tn), BF16), pltpu.SemaphoreType.DMA(())],
        compiler_params=_params(2),
        name=name,
    )(*a_parts, w)


def _na_geometry(rows, rb):
    kh = min(NA_KH, rows)
    kr = min(rb + kh, rows)
    nblk = rows // rb
    slab = np.clip(np.arange(nblk) * rb - kh // 2, 0, rows - kr)
    variants, variant_of = [], []
    for j in range(nblk):
        geo = []
        for a in range(rb):
            r = j * rb + a
            rs = int(np.clip(r - kh // 2, 0, rows - kh))
            lo = rs - int(slab[j])
            geo.append((lo, lo + kh, rs - r + NA_KH - 1))
        geo = tuple(geo)
        if geo not in variants:
            variants.append(geo)
        variant_of.append(variants.index(geo))
    return kr, slab, np.asarray(variant_of), variants


def _na_bias_table(rpb):
    h = rpb.shape[0]
    qcol = np.arange(GRID_W)
    cstart = np.clip(qcol - NA_KW // 2, 0, GRID_W - NA_KW)
    kcol = np.arange(GRID_W)
    col_ok = (kcol[None, :] >= cstart[:, None]) & (kcol[None, :] < cstart[:, None] + NA_KW)
    ext = GRID_W - NA_KW
    padded = jnp.pad(rpb.astype(F32), ((0, 0), (0, 0), (ext, ext)))
    e = jnp.stack([padded[:, :, GRID_W - 1 - qc:2 * GRID_W - 1 - qc] for qc in range(GRID_W)], axis=1)
    e = jnp.where(jnp.asarray(col_ok)[None, :, None, :], e, NEG_INF)
    e = e.reshape(h, GRID_W, (2 * NA_KH - 1) * GRID_W)
    return jnp.pad(e, ((0, 0), (0, 0), (0, GRID_W)), constant_values=NEG_INF)


def _nt_dot(a, b):
    return lax.dot_general(a, b, (((1,), (1,)), ((), ())), preferred_element_type=F32)


def _na_kernel(slab_ref, var_ref, q_ref, k_ref, v_ref, kc_ref, vc_ref, e_ref, o_ref, bias_ref,
               *, kr, hp, variants):
    j = pl.program_id(2)
    gw = GRID_W
    ks = pl.multiple_of(slab_ref[j] * gw, gw)
    nk = kr * gw
    hd = HEAD_DIM
    scale = hd ** -0.5

    changed = (j == 0) | (var_ref[j] != var_ref[jnp.maximum(j - 1, 0)])
    for vi, geo in enumerate(variants):
        @pl.when(changed & (var_ref[j] == vi))
        def _(geo=geo):
            for hh in range(hp):
                for a, (lo, hi, rr0) in enumerate(geo):
                    qrows = slice(a * gw, (a + 1) * gw)
                    if lo > 0:
                        bias_ref[hh, qrows, 0:lo * gw] = jnp.full((gw, lo * gw), NEG_INF, F32)
                    bias_ref[hh, qrows, lo * gw:hi * gw] = e_ref[hh, :, rr0 * gw:(rr0 + hi - lo) * gw]
                    if hi < kr:
                        bias_ref[hh, qrows, hi * gw:] = jnp.full((gw, (kr - hi) * gw), NEG_INF, F32)

    for hh in range(hp):
        cols = slice(hh * hd, (hh + 1) * hd)
        q = q_ref[:, cols]
        s_loc = _nt_dot(q, k_ref[pl.ds(ks, nk), cols]) * scale + bias_ref[hh]
        s_ctx = _nt_dot(q, kc_ref[:, cols]) * scale
        m = jnp.maximum(jnp.max(s_loc, axis=-1, keepdims=True), jnp.max(s_ctx, axis=-1, keepdims=True))
        p_loc = jnp.exp(s_loc - m)
        p_ctx = jnp.exp(s_ctx - m)
        v_ext = jnp.concatenate([v_ref[pl.ds(ks, nk), cols], jnp.ones((nk, hd), BF16)], axis=1)
        vc_ext = jnp.concatenate([vc_ref[:, cols], jnp.ones((vc_ref.shape[0], hd), BF16)], axis=1)
        oe = (jnp.dot(p_loc.astype(BF16), v_ext, preferred_element_type=F32)
              + jnp.dot(p_ctx.astype(BF16), vc_ext, preferred_element_type=F32))
        o_ref[:, cols] = (oe[:, :hd] / oe[:, hd:hd + 1]).astype(o_ref.dtype)


def _neighborhood_attn(p, pc, rpb, n_heads):
    b, t, _ = p.shape
    l = pc.shape[1]
    rows = t // GRID_W
    rb = min(NA_ROW_BLOCK, rows)
    hp = min(NA_HEADS_PER_STEP, n_heads)
    assert n_heads % hp == 0
    ng = n_heads // hp
    kr, slab, variant_of, variants = _na_geometry(rows, rb)
    table = _na_bias_table(rpb)
    nblk = rows // rb
    tq = rb * GRID_W
    w = hp * HEAD_DIM
    grid_spec = pltpu.PrefetchScalarGridSpec(
        num_scalar_prefetch=2,
        grid=(b, ng, nblk),
        in_specs=[
            pl.BlockSpec((None, tq, w), lambda i, h, j, s, v: (i, j, h)),
            pl.BlockSpec((None, t, w), lambda i, h, j, s, v: (i, 0, ng + h)),
            pl.BlockSpec((None, t, w), lambda i, h, j, s, v: (i, 0, 2 * ng + h)),
            pl.BlockSpec((None, l, w), lambda i, h, j, s, v: (i, 0, ng + h)),
            pl.BlockSpec((None, l, w), lambda i, h, j, s, v: (i, 0, 2 * ng + h)),
            pl.BlockSpec((hp, GRID_W, table.shape[2]), lambda i, h, j, s, v: (h, 0, 0)),
        ],
        out_specs=pl.BlockSpec((None, tq, w), lambda i, h, j, s, v: (i, j, h)),
        scratch_shapes=[pltpu.VMEM((hp, tq, kr * GRID_W), F32)],
    )
    return pl.pallas_call(
        functools.partial(_na_kernel, kr=kr, hp=hp, variants=tuple(variants)),
        grid_spec=grid_spec,
        out_shape=jax.ShapeDtypeStruct((b, t, n_heads * HEAD_DIM), BF16),
        compiler_params=_params(3),
        name="neighborhood_attn",
    )(jnp.asarray(slab, jnp.int32), jnp.asarray(variant_of, jnp.int32), p, p, p, pc, pc, table)


def _rope_tables(s):
    tpos = jnp.arange(s)
    row = (tpos // GRID_W).astype(F32)
    col = (tpos % GRID_W).astype(F32)
    nf = HEAD_DIM // 4
    inv = ROPE_THETA ** (-jnp.arange(nf, dtype=F32) / nf)
    lane = np.arange(HEAD_DIM)
    axis = lane // (2 * nf)
    second = (lane % (2 * nf)) >= nf
    freq = lane % nf
    pos = jnp.where(jnp.asarray(axis)[None, :] == 0, row[:, None], col[:, None])
    ang = pos * inv[jnp.asarray(freq)][None, :]
    cos, sin = jnp.cos(ang), jnp.sin(ang)
    sec = jnp.asarray(second)[None, :]
    return cos, jnp.where(sec, 0.0, -sin), jnp.where(sec, sin, 0.0)


def _rope(x, c, s1, s2):
    nf = HEAD_DIM // 4
    return x * c + pltpu.roll(x, HEAD_DIM - nf, axis=1) * s1 + pltpu.roll(x, nf, axis=1) * s2


def _wa_kernel(sink_ref, q_ref, k_ref, v_ref, kc_ref, vc_ref, c_ref, s1_ref, s2_ref, o_ref,
               *, group, kp, seq):
    kg = pl.program_id(1)
    n = pl.program_id(2)
    wb, hd = WA_BLOCK, HEAD_DIM
    nl = 3 * wb
    q0 = pl.multiple_of(n * wb, wb)
    bs = pl.multiple_of(jnp.clip(n * wb - wb, 0, seq - nl), wb)
    scale = hd ** -0.5
    cq, s1q, s2q = c_ref[pl.ds(q0, wb), :], s1_ref[pl.ds(q0, wb), :], s2_ref[pl.ds(q0, wb), :]
    ck, s1k, s2k = c_ref[pl.ds(bs, nl), :], s1_ref[pl.ds(bs, nl), :], s2_ref[pl.ds(bs, nl), :]
    rowi = lax.broadcasted_iota(jnp.int32, (group * wb, nl), 0)
    kpos = bs + lax.broadcasted_iota(jnp.int32, (group * wb, nl), 1)
    qpos = q0 + (rowi & (wb - 1))
    in_window = jnp.abs(kpos - qpos) <= WA_WINDOW
    gi = lax.broadcasted_iota(jnp.int32, (group * wb, 1), 0) // wb
    for kk in range(kp):
        kcols = slice(kk * hd, (kk + 1) * hd)
        q = jnp.concatenate(
            [_rope(q_ref[:, (kk * group + g) * hd:(kk * group + g + 1) * hd].astype(F32), cq, s1q, s2q)
             .astype(BF16) for g in range(group)], axis=0)
        kb = _rope(k_ref[pl.ds(bs, nl), kcols].astype(F32), ck, s1k, s2k).astype(BF16)
        s_loc = jnp.where(in_window, _nt_dot(q, kb) * scale, NEG_INF)
        s_ctx = _nt_dot(q, kc_ref[:, kcols]) * scale
        sink = jnp.zeros((group * wb, 1), F32)
        for g in range(group):
            sink = jnp.where(gi == g, sink_ref[(kg * kp + kk) * group + g], sink)
        m = jnp.maximum(jnp.maximum(jnp.max(s_loc, axis=-1, keepdims=True),
                                    jnp.max(s_ctx, axis=-1, keepdims=True)), sink)
        p_loc = jnp.exp(s_loc - m)
        p_ctx = jnp.exp(s_ctx - m)
        den = (jnp.sum(p_loc, axis=-1, keepdims=True) + jnp.sum(p_ctx, axis=-1, keepdims=True)
               + jnp.exp(sink - m))
        o = (jnp.dot(p_loc.astype(BF16), v_ref[pl.ds(bs, nl), kcols], preferred_element_type=F32)
             + jnp.dot(p_ctx.astype(BF16), vc_ref[:, kcols], preferred_element_type=F32))
        o = (o / den).astype(o_ref.dtype)
        for g in range(group):
            o_ref[:, (kk * group + g) * hd:(kk * group + g + 1) * hd] = o[g * wb:(g + 1) * wb, :]


def _window_gqa(p, pc, sink, q_col, k_col, v_col, n_heads, rope):
    b, t, _ = p.shape
    l = pc.shape[1]
    kvh = WA_KV_HEADS
    group = n_heads // kvh
    kp = min(WA_KV_PER_STEP, kvh)
    hd, wb = HEAD_DIM, WA_BLOCK
    qw = kp * group * hd
    kw = kp * hd
    assert kvh % kp == 0 and q_col % qw == 0 and k_col % kw == 0 and v_col % kw == 0
    assert t % wb == 0 and t >= 3 * wb
    cos, s1, s2 = rope
    tab_spec = pl.BlockSpec((t, hd), lambda i, k, n: (0, 0))
    return pl.pallas_call(
        functools.partial(_wa_kernel, group=group, kp=kp, seq=t),
        grid=(b, kvh // kp, t // wb),
        in_specs=[
            pl.BlockSpec(memory_space=pltpu.SMEM),
            pl.BlockSpec((None, wb, qw), lambda i, k, n: (i, n, q_col // qw + k)),
            pl.BlockSpec((None, t, kw), lambda i, k, n: (i, 0, k_col // kw + k)),
            pl.BlockSpec((None, t, kw), lambda i, k, n: (i, 0, v_col // kw + k)),
            pl.BlockSpec((None, l, kw), lambda i, k, n: (i, 0, k_col // kw + k)),
            pl.BlockSpec((None, l, kw), lambda i, k, n: (i, 0, v_col // kw + k)),
            tab_spec, tab_spec, tab_spec,
        ],
        out_specs=pl.BlockSpec((None, wb, qw), lambda i, k, n: (i, n, k)),
        out_shape=jax.ShapeDtypeStruct((b, t, n_heads * hd), BF16),
        compiler_params=_params(3),
        name="window_gqa",
    )(sink, p, p, p, pc, pc, cos, s1, s2)


def _ctx_kernel(sink_ref, q_ref, *refs, hp):
    k_refs, v_refs, o_ref = refs[:hp], refs[hp:2 * hp], refs[2 * hp]
    hd = HEAD_DIM
    scale = hd ** -0.5
    for hh in range(hp):
        cols = slice(hh * hd, (hh + 1) * hd)
        s = _nt_dot(q_ref[:, cols], k_refs[hh][...]) * scale
        sink = sink_ref[pl.program_id(1) * hp + hh]
        m = jnp.maximum(jnp.max(s, axis=-1, keepdims=True), sink)
        p = jnp.exp(s - m)
        den = jnp.sum(p, axis=-1, keepdims=True) + jnp.exp(sink - m)
        o = jnp.dot(p.astype(BF16), v_refs[hh][...], preferred_element_type=F32)
        o_ref[:, cols] = (o / den).astype(o_ref.dtype)


def _ctx_attn(pc, sink, na_heads, wa_heads):
    b, l, _ = pc.shape
    hd = HEAD_DIM
    group = wa_heads // WA_KV_HEADS
    hp = CTX_HEADS_PER_STEP
    while na_heads % hp or wa_heads % hp:
        hp //= 2
    qb0 = 3 * na_heads
    kb0 = qb0 + wa_heads
    vb0 = kb0 + WA_KV_HEADS
    sink_all = jnp.concatenate([jnp.full((na_heads,), NEG_INF, F32), sink.astype(F32)])

    def qmap(i, g):
        return (i, 0, jnp.where(g * hp < na_heads, g, (qb0 - na_heads) // hp + g))

    def kv_spec(hh, mha0, gqa0):
        def index(i, g):
            h = g * hp + hh
            return (i, 0, jnp.where(h < na_heads, mha0 + h, gqa0 + (h - na_heads) // group))
        return pl.BlockSpec((None, l, hd), index)

    return pl.pallas_call(
        functools.partial(_ctx_kernel, hp=hp),
        grid=(b, (na_heads + wa_heads) // hp),
        in_specs=[pl.BlockSpec(memory_space=pltpu.SMEM), pl.BlockSpec((None, l, hp * hd), qmap)]
        + [kv_spec(hh, na_heads, kb0) for hh in range(hp)]
        + [kv_spec(hh, 2 * na_heads, vb0) for hh in range(hp)],
        out_specs=pl.BlockSpec((None, l, hp * hd), lambda i, g: (i, 0, g)),
        out_shape=jax.ShapeDtypeStruct((b, l, (na_heads + wa_heads) * hd), BF16),
        compiler_params=_params(2),
        name="context_attn",
    )(sink_all, pc, *([pc] * (2 * hp)))


DFT_RADIX = 4
def _dft_cos_sin(n, ks, nt):
    lo = 1
    while lo * lo < nt:
        lo *= 2
    hi = nt // lo
    k = jnp.asarray(ks, dtype=jnp.int32)
    a = (2.0 * np.pi / n) * ((k[:, None] * (jnp.arange(hi, dtype=jnp.int32) * lo)[None, :]) % n).astype(F32)
    bb = (2.0 * np.pi / n) * ((k[:, None] * jnp.arange(lo, dtype=jnp.int32)[None, :]) % n).astype(F32)
    ca, sa, cb, sb = jnp.cos(a)[:, :, None], jnp.sin(a)[:, :, None], jnp.cos(bb)[:, None, :], jnp.sin(bb)[:, None, :]
    return (ca * cb - sa * sb).reshape(len(ks), nt), (sa * cb + ca * sb).reshape(len(ks), nt)


def _group_dft_kernel(u0_ref, u1_ref, u2_ref, u3_ref, cs_ref, o_ref, *, groups):
    gd = FNET_GROUP_DIM
    cs = cs_ref[...]
    for g in range(groups):
        cols = slice(g * gd, (g + 1) * gd)
        z = [jnp.dot(u_ref[:, cols], cs, preferred_element_type=F32) for u_ref in (u0_ref, u1_ref, u2_ref, u3_ref)]
        (a0, b0), (a1, b1), (a2, b2), (a3, b3) = [(zq[:, :gd], zq[:, gd:]) for zq in z]
        w_parts = [
            (a0 + a1 + a2 + a3, b0 + b1 + b2 + b3),
            (a0 - b1 - a2 + b3, b0 + a1 - b2 - a3),
            (a0 - a1 + a2 - a3, b0 - b1 + b2 - b3),
            (a0 + b1 - a2 - b3, b0 - a1 - b2 + a3),
        ]
        for r, (re, im) in enumerate(w_parts):
            o_ref[r, 0, :, cols] = re.astype(o_ref.dtype)
            o_ref[r, 1, :, cols] = im.astype(o_ref.dtype)


def _group_dft(p, width):
    b, t, _ = p.shape
    gd = FNET_GROUP_DIM
    c, s = _dft_cos_sin(gd, np.arange(gd), gd)
    cs = (jnp.concatenate([c, s], axis=1) * (gd ** -0.5)).astype(BF16)
    tq = t // DFT_RADIX
    tt = _pick(tq, 256)
    nq = tq // tt

    def rows(q):
        return pl.BlockSpec((None, tt, width), lambda i, j: (i, q * nq + j, 0))

    return pl.pallas_call(
        functools.partial(_group_dft_kernel, groups=width // gd),
        grid=(b, nq),
        in_specs=[rows(q) for q in range(DFT_RADIX)] + [pl.BlockSpec((gd, 2 * gd), lambda i, j: (0, 0))],
        out_specs=pl.BlockSpec((None, DFT_RADIX, 2, tt, width), lambda i, j: (i, 0, 0, j, 0)),
        out_shape=jax.ShapeDtypeStruct((b, DFT_RADIX, 2, tq, width), BF16),
        compiler_params=_params(2),
        name="group_dft",
    )(p, p, p, p, cs)


def _seq_dft_kernel(a_ref, v_ref, perm_ref, o_ref, *, radix, sub):
    tm = a_ref.shape[1]
    ys = [jnp.dot(a_ref[r], v_ref[r], preferred_element_type=F32).astype(BF16) for r in range(radix)]
    perm = perm_ref[...]
    for g in range(tm // sub):
        stacked = jnp.concatenate([y[g * sub:(g + 1) * sub, :] for y in ys], axis=0)
        o_ref[g * radix * sub:(g + 1) * radix * sub, :] = jnp.dot(
            perm, stacked, preferred_element_type=F32).astype(o_ref.dtype)


def _seq_dft(v):
    b, radix, _, tq, w = v.shape
    t = radix * tq
    mats = []
    for r in range(radix):
        c, s = _dft_cos_sin(t, np.arange(r, t, radix), tq)
        mats.append(jnp.concatenate([c, -s], axis=1) * (t ** -0.5))
    a = jnp.stack(mats).astype(BF16)
    tm = _pick(tq, 512)
    tn = _pick(w, 512)
    sub = min(tm, 2 * LANES // radix)
    rows = np.arange(radix * sub)
    perm = np.zeros((radix * sub, radix * sub), np.float32)
    perm[rows, (rows % radix) * sub + rows // radix] = 1.0
    return pl.pallas_call(
        functools.partial(_seq_dft_kernel, radix=radix, sub=sub),
        grid=(b, w // tn, tq // tm),
        in_specs=[pl.BlockSpec((radix, tm, 2 * tq), lambda g, j, i: (0, i, 0)),
                  pl.BlockSpec((None, radix, 2 * tq, tn), lambda g, j, i: (g, 0, 0, j)),
                  pl.BlockSpec((radix * sub, radix * sub), lambda g, j, i: (0, 0))],
        out_specs=pl.BlockSpec((None, radix * tm, tn), lambda g, j, i: (g, i, j)),
        out_shape=jax.ShapeDtypeStruct((b, t, w), BF16),
        compiler_params=_params(3),
        name="seq_dft",
    )(a, v.reshape(b, radix, 2 * tq, w), jnp.asarray(perm, BF16))


def _conv_cols(t, width):
    tc = LANES
    while tc * 2 * t <= CONV_BLOCK_ELEMS and width % (tc * 2) == 0:
        tc *= 2
    return tc


def _dwconv3(z, w_ref):
    t = z.shape[0]
    sub = min(F32_SUBLANES, t)
    row = lax.broadcasted_iota(jnp.int32, (sub, z.shape[1]), 0)
    zp = pltpu.roll(z, 1, axis=0)
    zn = pltpu.roll(z, t - 1, axis=0)
    zp = jnp.concatenate([jnp.where(row == 0, 0.0, zp[:sub]), zp[sub:]], axis=0)
    zn = jnp.concatenate([zn[:t - sub], jnp.where(row == sub - 1, 0.0, zn[t - sub:])], axis=0)
    return zp * w_ref[0:1, :] + z * w_ref[1:2, :] + zn * w_ref[2:3, :]


def _shortconv_kernel(bg_ref, cg_ref, hv_ref, w_ref, o_ref):
    z = cg_ref[...].astype(F32) * hv_ref[...].astype(F32)
    o_ref[...] = (bg_ref[...].astype(F32) * _dwconv3(z, w_ref)).astype(o_ref.dtype)


def _shortconv(p, conv_w, layer, col0, width):
    b, t, _ = p.shape
    tc = _conv_cols(t, math.gcd(width, col0))
    nb = width // tc
    c0 = col0 // tc
    return pl.pallas_call(
        _shortconv_kernel,
        grid=(b, nb),
        in_specs=[pl.BlockSpec((None, t, tc), lambda i, j: (i, 0, c0 + j)),
                  pl.BlockSpec((None, t, tc), lambda i, j: (i, 0, c0 + nb + j)),
                  pl.BlockSpec((None, t, tc), lambda i, j: (i, 0, c0 + 2 * nb + j)),
                  pl.BlockSpec((None, 3, tc), lambda i, j: (layer, 0, j))],
        out_specs=pl.BlockSpec((None, t, tc), lambda i, j: (i, 0, j)),
        out_shape=jax.ShapeDtypeStruct((b, t, width), BF16),
        compiler_params=_params(2),
        name="shortconv_gate",
    )(p, p, p, conv_w)


def _ffn_gate_kernel(g_ref, v_ref, wg_ref, wv_ref, o_ref):
    gate = _dwconv3(g_ref[...].astype(F32), wg_ref)
    val = _dwconv3(v_ref[...].astype(F32), wv_ref)
    o_ref[...] = (gate * val / (1.0 + jnp.exp2(gate * (-LOG2E)))).astype(o_ref.dtype)


def _ffn_gate(u, conv_w, layer):
    b, t, f2 = u.shape
    f = f2 // 2
    tc = _conv_cols(t, f)
    nb = f // tc
    return pl.pallas_call(
        _ffn_gate_kernel,
        grid=(b, nb),
        in_specs=[pl.BlockSpec((None, t, tc), lambda i, j: (i, 0, j)),
                  pl.BlockSpec((None, t, tc), lambda i, j: (i, 0, nb + j)),
                  pl.BlockSpec((None, 3, tc), lambda i, j: (layer, 0, j)),
                  pl.BlockSpec((None, 3, tc), lambda i, j: (layer, 0, nb + j))],
        out_specs=pl.BlockSpec((None, t, tc), lambda i, j: (i, 0, j)),
        out_shape=jax.ShapeDtypeStruct((b, t, f), BF16),
        compiler_params=_params(2),
        name="ffn_conv_gate",
    )(u, u, conv_w, conv_w)


def _proj(parts, w, layer, out_dtype, **kw):
    b, t, _ = parts[0].shape
    y = _matmul([a.reshape(b * t, a.shape[2]) for a in parts], w, layer, out_dtype, **kw)
    return y.reshape(b, t, -1)


def _even_mixer(hl, hc, w_in, rpb, sink, w_out, e, rope, ctx_live):
    na_heads = rpb.shape[1]
    wa_heads = sink.shape[1]
    na_w = na_heads * HEAD_DIM
    wa_w = wa_heads * HEAD_DIM
    kv_w = WA_KV_HEADS * HEAD_DIM
    p = _proj([hl], w_in, e, BF16, name="even_w_in")
    pc = _proj([hc], w_in, e, BF16, name="even_w_in_ctx")
    oa = _neighborhood_attn(p, pc, rpb[e], na_heads)
    ob = _window_gqa(p, pc, sink[e].astype(F32), 3 * na_w, 3 * na_w + wa_w, 3 * na_w + wa_w + kv_w,
                     wa_heads, rope)
    yl = _proj([oa, ob], w_out, e, BF16, name="even_w_out")
    yc = None
    if ctx_live:
        yc = _proj([_ctx_attn(pc, sink[e], na_heads, wa_heads)], w_out, e, BF16, name="even_w_out_ctx")
    return yl, yc


def _odd_mixer(h, w_in, conv_w, w_out, o):
    sc_w = conv_w.shape[2]
    f_w = w_in.shape[2] - 3 * sc_w
    p = _proj([h], w_in, o, BF16, name="odd_w_in")
    f = _seq_dft(_group_dft(p, f_w))
    sc = _shortconv(p, conv_w, o, f_w, sc_w)
    return _proj([f, sc], w_out, o, BF16, name="odd_w_out")


def _conv_ffn(h, w_up, conv_w, w_down, i):
    u = _proj([h], w_up, i, BF16, name="ffn_w_up")
    a = _ffn_gate(u, conv_w, i)
    return _proj([a], w_down, i, BF16, tm=512, tn=1024, name="ffn_w_down")


def kernel(x, c, ctx, c_ctx, w_mod, b_mod, g_mix_pre, g_mix_post, g_ffn_pre, g_ffn_post, even_w_in, even_rpb, even_sink, even_w_out, odd_w_in, odd_conv, odd_w_out, ffn_w_up, ffn_conv, ffn_w_down):
    b, t, d = x.shape
    depth = w_mod.shape[0]
    rows = -(-(b + 1) // 8) * 8
    cond = jnp.zeros((rows, d), F32).at[:b].set(c).at[b].set(c_ctx)
    mod = _modulation(cond, w_mod, b_mod)
    rope = _rope_tables(t)
    xc = ctx
    hl = _prenorm(x, g_mix_pre[0], mod, 0, 0)
    for i in range(depth):
        ctx_live = any(j % 2 == 0 for j in range(i + 1, depth))
        need_hc = (i % 2 == 0) or ctx_live
        hc = _prenorm(xc, g_mix_pre[i], mod, i, 0, fixed_row=b) if need_hc else None
        if i % 2 == 0:
            e = i // 2
            yl, yc = _even_mixer(hl, hc, even_w_in, even_rpb, even_sink, even_w_out, e, rope, ctx_live)
        else:
            o = i // 2
            yl = _odd_mixer(hl, odd_w_in, odd_conv, odd_w_out, o)
            yc = _odd_mixer(hc, odd_w_in, odd_conv, odd_w_out, o) if ctx_live else None
        x, hl = _residual(x, yl, g_mix_post[i], mod, i, 2, nxt=(g_ffn_pre[i], i, 3))
        yf = _conv_ffn(hl, ffn_w_up, ffn_conv, ffn_w_down, i)
        nxt = (g_mix_pre[i + 1], i + 1, 0) if i + 1 < depth else None
        x, hl = _residual(x, yf, g_ffn_post[i], mod, i, 5, nxt=nxt)
        if ctx_live:
            xc, hcf = _residual(xc, yc, g_mix_post[i], mod, i, 2, nxt=(g_ffn_pre[i], i, 3), fixed_row=b)
            yfc = _conv_ffn(hcf, ffn_w_up, ffn_conv, ffn_w_down, i)
            xc, _ = _residual(xc, yfc, g_ffn_post[i], mod, i, 5, fixed_row=b)
    return x
```

```python
import functools
import math

import numpy as np
import jax
import jax.numpy as jnp
from jax import lax
from jax.experimental import pallas as pl
from jax.experimental.pallas import tpu as pltpu

F32 = jnp.float32
BF16 = jnp.bfloat16

GRID_W = 64
HEAD_DIM = 128
NA_KH = 8
NA_KW = 16
WA_KV_HEADS = 4
WA_WINDOW = 128
WA_BLOCK = 128
FNET_GROUP_DIM = 128
ROPE_THETA = 10000.0
RMS_EPS = 1e-6
NEG_INF = -1e30
LOG2E = 1.4426950408889634

V7X_VMEM_BYTES = 64 * 1024 * 1024
VMEM_LIMIT = V7X_VMEM_BYTES - 8 * 1024 * 1024
LANES = 128

NA_ROW_BLOCK = 4
NA_HEADS_PER_STEP = 8
WA_KV_PER_STEP = 4
CTX_HEADS_PER_STEP = 4
CONV_BLOCK_ELEMS = 4096 * 128
F32_SUBLANES = 8
MIN_ROW_TILES_FOR_STAGING = 8


def _params(n_grid):
    return pltpu.CompilerParams(dimension_semantics=("arbitrary",) * n_grid,
                                vmem_limit_bytes=VMEM_LIMIT)


def _pick(n, pref):
    if n <= pref:
        return n
    t = pref
    while n % t:
        t //= 2
    return t


def _mod_kernel(c_ref, w_ref, b_ref, o_ref):
    c = c_ref[...]
    s = (c / (1.0 + jnp.exp(-c))).astype(BF16)
    y = jnp.dot(s, w_ref[...].astype(BF16), preferred_element_type=F32)
    o_ref[...] = y + b_ref[...]


def _modulation(cond, w_mod, b_mod):
    depth, d, n = w_mod.shape
    rows = cond.shape[0]
    tn = _pick(n, 512)
    return pl.pallas_call(
        _mod_kernel,
        grid=(depth, n // tn),
        in_specs=[pl.BlockSpec((rows, d), lambda l, j: (0, 0)),
                  pl.BlockSpec((None, d, tn), lambda l, j: (l, 0, j)),
                  pl.BlockSpec((None, 1, tn), lambda l, j: (l, 0, j))],
        out_specs=pl.BlockSpec((None, rows, tn), lambda l, j: (l, 0, j)),
        out_shape=jax.ShapeDtypeStruct((depth, rows, n), F32),
        compiler_params=_params(2),
        name="adaln_mod",
    )(cond, w_mod, b_mod.reshape(depth, 1, n))


def _rms(v, g):
    return v * lax.rsqrt(jnp.mean(v * v, axis=-1, keepdims=True) + RMS_EPS) * g


def _mod_row(ref, fixed_row):
    row = pl.program_id(0) if fixed_row is None else fixed_row
    return ref[pl.ds(row, 1), :]


def _prenorm_kernel(x_ref, g_ref, sh_ref, sc_ref, h_ref, *, fixed_row):
    sh = _mod_row(sh_ref, fixed_row)
    sc = _mod_row(sc_ref, fixed_row)
    h_ref[...] = (_rms(x_ref[...], g_ref[...]) * (1.0 + sc) + sh).astype(h_ref.dtype)


def _prenorm(x, g, mod, layer, slot, fixed_row=None):
    b, t, d = x.shape
    rows = mod.shape[1]
    tr = _pick(t, 256)
    return pl.pallas_call(
        functools.partial(_prenorm_kernel, fixed_row=fixed_row),
        grid=(b, t // tr),
        in_specs=[pl.BlockSpec((None, tr, d), lambda i, j: (i, j, 0)),
                  pl.BlockSpec((1, d), lambda i, j: (0, 0)),
                  pl.BlockSpec((None, rows, d), lambda i, j: (layer, 0, slot)),
                  pl.BlockSpec((None, rows, d), lambda i, j: (layer, 0, slot + 1))],
        out_specs=pl.BlockSpec((None, tr, d), lambda i, j: (i, j, 0)),
        out_shape=jax.ShapeDtypeStruct((b, t, d), BF16),
        compiler_params=_params(2),
        name="prenorm",
    )(x, g.reshape(1, d), mod, mod)


def _resid_kernel(x_ref, y_ref, gp_ref, gt_ref, *rest, fixed_row, with_next):
    gt = _mod_row(gt_ref, fixed_row)
    xn = x_ref[...] + gt * _rms(y_ref[...].astype(F32), gp_ref[...])
    if with_next:
        gn_ref, sh_ref, sc_ref, xo_ref, h_ref = rest
        sh = _mod_row(sh_ref, fixed_row)
        sc = _mod_row(sc_ref, fixed_row)
        h_ref[...] = (_rms(xn, gn_ref[...]) * (1.0 + sc) + sh).astype(h_ref.dtype)
    else:
        (xo_ref,) = rest
    xo_ref[...] = xn


def _residual(x, y, g_post, mod, layer, gate_slot, nxt=None, fixed_row=None):
    b, t, d = x.shape
    rows = mod.shape[1]
    tr = _pick(t, 256)
    row_spec = pl.BlockSpec((None, tr, d), lambda i, j: (i, j, 0))
    vec_spec = pl.BlockSpec((1, d), lambda i, j: (0, 0))

    def mod_spec(l, s):
        return pl.BlockSpec((None, rows, d), lambda i, j: (l, 0, s))

    in_specs = [row_spec, row_spec, vec_spec, mod_spec(layer, gate_slot)]
    args = [x, y, g_post.reshape(1, d), mod]
    out_specs = [row_spec]
    out_shape = [jax.ShapeDtypeStruct((b, t, d), F32)]
    if nxt is not None:
        g_pre, nl, ns = nxt
        in_specs += [vec_spec, mod_spec(nl, ns), mod_spec(nl, ns + 1)]
        args += [g_pre.reshape(1, d), mod, mod]
        out_specs.append(row_spec)
        out_shape.append(jax.ShapeDtypeStruct((b, t, d), BF16))
    out = pl.pallas_call(
        functools.partial(_resid_kernel, fixed_row=fixed_row, with_next=nxt is not None),
        grid=(b, t // tr),
        in_specs=in_specs, out_specs=out_specs, out_shape=out_shape,
        compiler_params=_params(2),
        name="residual_norm",
    )(*args)
    return (out[0], out[1]) if nxt is not None else (out[0], None)


def _dot_parts(a_refs, w_ref, o_ref):
    acc = None
    k0 = 0
    for a_ref in a_refs:
        kp = a_ref.shape[1]
        part = jnp.dot(a_ref[...], w_ref[k0:k0 + kp, :], preferred_element_type=F32)
        acc = part if acc is None else acc + part
        k0 += kp
    o_ref[...] = acc.astype(o_ref.dtype)


def _mm_cast_kernel(*refs, n_parts):
    a_refs = refs[:n_parts]
    w_ref, o_ref, wbf = refs[n_parts:]

    @pl.when(pl.program_id(1) == 0)
    def _():
        wbf[...] = w_ref[...].astype(BF16)

    _dot_parts(a_refs, wbf, o_ref)


def _mm_staged_kernel(*refs, n_parts, layer, tn):
    a_refs = refs[:n_parts]
    w_hbm, o_ref, stage, wbf, sem = refs[n_parts:]
    j = pl.program_id(0)

    def tile_copy(jj):
        return pltpu.make_async_copy(w_hbm.at[layer, :, pl.ds(pl.multiple_of(jj * tn, tn), tn)], stage, sem)

    @pl.when(pl.program_id(1) == 0)
    def _():
        @pl.when(j == 0)
        def _():
            tile_copy(0).start()

        tile_copy(j).wait()
        wbf[...] = stage[...].astype(BF16)

        @pl.when(j + 1 < pl.num_programs(0))
        def _():
            tile_copy(j + 1).start()

    _dot_parts(a_refs, wbf, o_ref)


def _matmul(a_parts, w, layer, out_dtype, *, tm=1024, tn=1024, name="matmul"):
    m = a_parts[0].shape[0]
    _, k, n = w.shape
    assert sum(a.shape[1] for a in a_parts) == k
    tm = _pick(m, tm)
    n_parts = len(a_parts)
    a_specs = [pl.BlockSpec((tm, a.shape[1]), lambda j, i: (i, 0)) for a in a_parts]
    out_shape = jax.ShapeDtypeStruct((m, n), out_dtype)
    if m // tm >= MIN_ROW_TILES_FOR_STAGING:
        tn = _pick(n, tn)
        return pl.pallas_call(
            functools.partial(_mm_staged_kernel, n_parts=n_parts, layer=layer, tn=tn),
            grid=(n // tn, m // tm),
            in_specs=a_specs + [pl.BlockSpec(memory_space=pl.ANY)],
            out_specs=pl.BlockSpec((tm, tn), lambda j, i: (i, j)),
            out_shape=out_shape,
            scratch_shapes=[pltpu.VMEM((k, tn), F32), pltpu.VMEM((k, tn), BF16), pltpu.SemaphoreType.DMA(())],
            compiler_params=_params(2),
            name=name,
        )(*a_parts, w)
    tn = _pick(n, tn // 2)
    return pl.pallas_call(
        functools.partial(_mm_cast_kernel, n_parts=n_parts),
        grid=(n // tn, m // tm),
        in_specs=a_specs + [pl.BlockSpec((None, k, tn), lambda j, i: (layer, 0, j))],
        out_specs=pl.BlockSpec((tm, tn), lambda j, i: (i, j)),
        out_shape=out_shape,
        scratch_shapes=[pltpu.VMEM((k, tn), BF16)],
        compiler_params=_params(2),
        name=name,
    )(*a_parts, w)


def _na_geometry(rows, rb):
    kh = min(NA_KH, rows)
    kr = min(rb + kh, rows)
    nblk = rows // rb
    slab = np.clip(np.arange(nblk) * rb - kh // 2, 0, rows - kr)
    variants, variant_of = [], []
    for j in range(nblk):
        geo = []
        for a in range(rb):
            r = j * rb + a
            rs = int(np.clip(r - kh // 2, 0, rows - kh))
            lo = rs - int(slab[j])
            geo.append((lo, lo + kh, rs - r + NA_KH - 1))
        geo = tuple(geo)
        if geo not in variants:
            variants.append(geo)
        variant_of.append(variants.index(geo))
    return kr, slab, np.asarray(variant_of), variants


def _na_bias_table(rpb):
    h = rpb.shape[0]
    qcol = np.arange(GRID_W)
    cstart = np.clip(qcol - NA_KW // 2, 0, GRID_W - NA_KW)
    kcol = np.arange(GRID_W)
    col_ok = (kcol[None, :] >= cstart[:, None]) & (kcol[None, :] < cstart[:, None] + NA_KW)
    ext = GRID_W - NA_KW
    padded = jnp.pad(rpb.astype(F32), ((0, 0), (0, 0), (ext, ext)))
    e = jnp.stack([padded[:, :, GRID_W - 1 - qc:2 * GRID_W - 1 - qc] for qc in range(GRID_W)], axis=1)
    e = jnp.where(jnp.asarray(col_ok)[None, :, None, :], e, NEG_INF)
    e = e.reshape(h, GRID_W, (2 * NA_KH - 1) * GRID_W)
    return jnp.pad(e, ((0, 0), (0, 0), (0, GRID_W)), constant_values=NEG_INF)


def _nt_dot(a, b):
    return lax.dot_general(a, b, (((1,), (1,)), ((), ())), preferred_element_type=F32)


def _na_kernel(slab_ref, var_ref, q_ref, k_ref, v_ref, kc_ref, vc_ref, e_ref, o_ref, bias_ref,
               *, kr, hp, variants):
    j = pl.program_id(2)
    gw = GRID_W
    ks = pl.multiple_of(slab_ref[j] * gw, gw)
    nk = kr * gw
    hd = HEAD_DIM
    scale = hd ** -0.5

    changed = (j == 0) | (var_ref[j] != var_ref[jnp.maximum(j - 1, 0)])
    for vi, geo in enumerate(variants):
        @pl.when(changed & (var_ref[j] == vi))
        def _(geo=geo):
            for hh in range(hp):
                for a, (lo, hi, rr0) in enumerate(geo):
                    qrows = slice(a * gw, (a + 1) * gw)
                    if lo > 0:
                        bias_ref[hh, qrows, 0:lo * gw] = jnp.full((gw, lo * gw), NEG_INF, F32)
                    bias_ref[hh, qrows, lo * gw:hi * gw] = e_ref[hh, :, rr0 * gw:(rr0 + hi - lo) * gw]
                    if hi < kr:
                        bias_ref[hh, qrows, hi * gw:] = jnp.full((gw, (kr - hi) * gw), NEG_INF, F32)

    for hh in range(hp):
        cols = slice(hh * hd, (hh + 1) * hd)
        q = q_ref[:, cols]
        s_loc = _nt_dot(q, k_ref[pl.ds(ks, nk), cols]) * scale + bias_ref[hh]
        s_ctx = _nt_dot(q, kc_ref[:, cols]) * scale
        m = jnp.maximum(jnp.max(s_loc, axis=-1, keepdims=True), jnp.max(s_ctx, axis=-1, keepdims=True))
        p_loc = jnp.exp(s_loc - m)
        p_ctx = jnp.exp(s_ctx - m)
        v_ext = jnp.concatenate([v_ref[pl.ds(ks, nk), cols], jnp.ones((nk, hd), BF16)], axis=1)
        vc_ext = jnp.concatenate([vc_ref[:, cols], jnp.ones((vc_ref.shape[0], hd), BF16)], axis=1)
        oe = (jnp.dot(p_loc.astype(BF16), v_ext, preferred_element_type=F32)
              + jnp.dot(p_ctx.astype(BF16), vc_ext, preferred_element_type=F32))
        o_ref[:, cols] = (oe[:, :hd] / oe[:, hd:hd + 1]).astype(o_ref.dtype)


def _neighborhood_attn(p, pc, rpb, n_heads):
    b, t, _ = p.shape
    l = pc.shape[1]
    rows = t // GRID_W
    rb = min(NA_ROW_BLOCK, rows)
    hp = min(NA_HEADS_PER_STEP, n_heads)
    assert n_heads % hp == 0
    ng = n_heads // hp
    kr, slab, variant_of, variants = _na_geometry(rows, rb)
    table = _na_bias_table(rpb)
    nblk = rows // rb
    tq = rb * GRID_W
    w = hp * HEAD_DIM
    grid_spec = pltpu.PrefetchScalarGridSpec(
        num_scalar_prefetch=2,
        grid=(b, ng, nblk),
        in_specs=[
            pl.BlockSpec((None, tq, w), lambda i, h, j, s, v: (i, j, h)),
            pl.BlockSpec((None, t, w), lambda i, h, j, s, v: (i, 0, ng + h)),
            pl.BlockSpec((None, t, w), lambda i, h, j, s, v: (i, 0, 2 * ng + h)),
            pl.BlockSpec((None, l, w), lambda i, h, j, s, v: (i, 0, ng + h)),
            pl.BlockSpec((None, l, w), lambda i, h, j, s, v: (i, 0, 2 * ng + h)),
            pl.BlockSpec((hp, GRID_W, table.shape[2]), lambda i, h, j, s, v: (h, 0, 0)),
        ],
        out_specs=pl.BlockSpec((None, tq, w), lambda i, h, j, s, v: (i, j, h)),
        scratch_shapes=[pltpu.VMEM((hp, tq, kr * GRID_W), F32)],
    )
    return pl.pallas_call(
        functools.partial(_na_kernel, kr=kr, hp=hp, variants=tuple(variants)),
        grid_spec=grid_spec,
        out_shape=jax.ShapeDtypeStruct((b, t, n_heads * HEAD_DIM), BF16),
        compiler_params=_params(3),
        name="neighborhood_attn",
    )(jnp.asarray(slab, jnp.int32), jnp.asarray(variant_of, jnp.int32), p, p, p, pc, pc, table)


def _rope_tables(s):
    tpos = jnp.arange(s)
    row = (tpos // GRID_W).astype(F32)
    col = (tpos % GRID_W).astype(F32)
    nf = HEAD_DIM // 4
    inv = ROPE_THETA ** (-jnp.arange(nf, dtype=F32) / nf)
    lane = np.arange(HEAD_DIM)
    axis = lane // (2 * nf)
    second = (lane % (2 * nf)) >= nf
    freq = lane % nf
    pos = jnp.where(jnp.asarray(axis)[None, :] == 0, row[:, None], col[:, None])
    ang = pos * inv[jnp.asarray(freq)][None, :]
    cos, sin = jnp.cos(ang), jnp.sin(ang)
    sec = jnp.asarray(second)[None, :]
    return cos, jnp.where(sec, 0.0, -sin), jnp.where(sec, sin, 0.0)


def _rope(x, c, s1, s2):
    nf = HEAD_DIM // 4
    return x * c + pltpu.roll(x, HEAD_DIM - nf, axis=1) * s1 + pltpu.roll(x, nf, axis=1) * s2


def _wa_kernel(sink_ref, q_ref, k_ref, v_ref, kc_ref, vc_ref, c_ref, s1_ref, s2_ref, o_ref,
               *, group, kp, seq):
    kg = pl.program_id(1)
    n = pl.program_id(2)
    wb, hd = WA_BLOCK, HEAD_DIM
    nl = 3 * wb
    q0 = pl.multiple_of(n * wb, wb)
    bs = pl.multiple_of(jnp.clip(n * wb - wb, 0, seq - nl), wb)
    scale = hd ** -0.5
    cq, s1q, s2q = c_ref[pl.ds(q0, wb), :], s1_ref[pl.ds(q0, wb), :], s2_ref[pl.ds(q0, wb), :]
    ck, s1k, s2k = c_ref[pl.ds(bs, nl), :], s1_ref[pl.ds(bs, nl), :], s2_ref[pl.ds(bs, nl), :]
    rowi = lax.broadcasted_iota(jnp.int32, (group * wb, nl), 0)
    kpos = bs + lax.broadcasted_iota(jnp.int32, (group * wb, nl), 1)
    qpos = q0 + (rowi & (wb - 1))
    in_window = jnp.abs(kpos - qpos) <= WA_WINDOW
    gi = lax.broadcasted_iota(jnp.int32, (group * wb, 1), 0) // wb
    for kk in range(kp):
        kcols = slice(kk * hd, (kk + 1) * hd)
        q = jnp.concatenate(
            [_rope(q_ref[:, (kk * group + g) * hd:(kk * group + g + 1) * hd].astype(F32), cq, s1q, s2q)
             .astype(BF16) for g in range(group)], axis=0)
        kb = _rope(k_ref[pl.ds(bs, nl), kcols].astype(F32), ck, s1k, s2k).astype(BF16)
        s_loc = jnp.where(in_window, _nt_dot(q, kb) * scale, NEG_INF)
        s_ctx = _nt_dot(q, kc_ref[:, kcols]) * scale
        sink = jnp.zeros((group * wb, 1), F32)
        for g in range(group):
            sink = jnp.where(gi == g, sink_ref[(kg * kp + kk) * group + g], sink)
        m = jnp.maximum(jnp.maximum(jnp.max(s_loc, axis=-1, keepdims=True),
                                    jnp.max(s_ctx, axis=-1, keepdims=True)), sink)
        p_loc = jnp.exp(s_loc - m)
        p_ctx = jnp.exp(s_ctx - m)
        den = (jnp.sum(p_loc, axis=-1, keepdims=True) + jnp.sum(p_ctx, axis=-1, keepdims=True)
               + jnp.exp(sink - m))
        o = (jnp.dot(p_loc.astype(BF16), v_ref[pl.ds(bs, nl), kcols], preferred_element_type=F32)
             + jnp.dot(p_ctx.astype(BF16), vc_ref[:, kcols], preferred_element_type=F32))
        o = (o / den).astype(o_ref.dtype)
        for g in range(group):
            o_ref[:, (kk * group + g) * hd:(kk * group + g + 1) * hd] = o[g * wb:(g + 1) * wb, :]


def _window_gqa(p, pc, sink, q_col, k_col, v_col, n_heads, rope):
    b, t, _ = p.shape
    l = pc.shape[1]
    kvh = WA_KV_HEADS
    group = n_heads // kvh
    kp = min(WA_KV_PER_STEP, kvh)
    hd, wb = HEAD_DIM, WA_BLOCK
    qw = kp * group * hd
    kw = kp * hd
    assert kvh % kp == 0 and q_col % qw == 0 and k_col % kw == 0 and v_col % kw == 0
    assert t % wb == 0 and t >= 3 * wb
    cos, s1, s2 = rope
    tab_spec = pl.BlockSpec((t, hd), lambda i, k, n: (0, 0))
    return pl.pallas_call(
        functools.partial(_wa_kernel, group=group, kp=kp, seq=t),
        grid=(b, kvh // kp, t // wb),
        in_specs=[
            pl.BlockSpec(memory_space=pltpu.SMEM),
            pl.BlockSpec((None, wb, qw), lambda i, k, n: (i, n, q_col // qw + k)),
            pl.BlockSpec((None, t, kw), lambda i, k, n: (i, 0, k_col // kw + k)),
            pl.BlockSpec((None, t, kw), lambda i, k, n: (i, 0, v_col // kw + k)),
            pl.BlockSpec((None, l, kw), lambda i, k, n: (i, 0, k_col // kw + k)),
            pl.BlockSpec((None, l, kw), lambda i, k, n: (i, 0, v_col // kw + k)),
            tab_spec, tab_spec, tab_spec,
        ],
        out_specs=pl.BlockSpec((None, wb, qw), lambda i, k, n: (i, n, k)),
        out_shape=jax.ShapeDtypeStruct((b, t, n_heads * hd), BF16),
        compiler_params=_params(3),
        name="window_gqa",
    )(sink, p, p, p, pc, pc, cos, s1, s2)


def _ctx_kernel(sink_ref, q_ref, *refs, hp):
    k_refs, v_refs, o_ref = refs[:hp], refs[hp:2 * hp], refs[2 * hp]
    hd = HEAD_DIM
    scale = hd ** -0.5
    for hh in range(hp):
        cols = slice(hh * hd, (hh + 1) * hd)
        s = _nt_dot(q_ref[:, cols], k_refs[hh][...]) * scale
        sink = sink_ref[pl.program_id(1) * hp + hh]
        m = jnp.maximum(jnp.max(s, axis=-1, keepdims=True), sink)
        p = jnp.exp(s - m)
        den = jnp.sum(p, axis=-1, keepdims=True) + jnp.exp(sink - m)
        o = jnp.dot(p.astype(BF16), v_refs[hh][...], preferred_element_type=F32)
        o_ref[:, cols] = (o / den).astype(o_ref.dtype)


def _ctx_attn(pc, sink, na_heads, wa_heads):
    b, l, _ = pc.shape
    hd = HEAD_DIM
    group = wa_heads // WA_KV_HEADS
    hp = CTX_HEADS_PER_STEP
    while na_heads % hp or wa_heads % hp:
        hp //= 2
    qb0 = 3 * na_heads
    kb0 = qb0 + wa_heads
    vb0 = kb0 + WA_KV_HEADS
    sink_all = jnp.concatenate([jnp.full((na_heads,), NEG_INF, F32), sink.astype(F32)])

    def qmap(i, g):
        return (i, 0, jnp.where(g * hp < na_heads, g, (qb0 - na_heads) // hp + g))

    def kv_spec(hh, mha0, gqa0):
        def index(i, g):
            h = g * hp + hh
            return (i, 0, jnp.where(h < na_heads, mha0 + h, gqa0 + (h - na_heads) // group))
        return pl.BlockSpec((None, l, hd), index)

    return pl.pallas_call(
        functools.partial(_ctx_kernel, hp=hp),
        grid=(b, (na_heads + wa_heads) // hp),
        in_specs=[pl.BlockSpec(memory_space=pltpu.SMEM), pl.BlockSpec((None, l, hp * hd), qmap)]
        + [kv_spec(hh, na_heads, kb0) for hh in range(hp)]
        + [kv_spec(hh, 2 * na_heads, vb0) for hh in range(hp)],
        out_specs=pl.BlockSpec((None, l, hp * hd), lambda i, g: (i, 0, g)),
        out_shape=jax.ShapeDtypeStruct((b, l, (na_heads + wa_heads) * hd), BF16),
        compiler_params=_params(2),
        name="context_attn",
    )(sink_all, pc, *([pc] * (2 * hp)))


DFT_RADIX = 4
def _dft_cos_sin(n, ks, nt):
    lo = 1
    while lo * lo < nt:
        lo *= 2
    hi = nt // lo
    k = jnp.asarray(ks, dtype=jnp.int32)
    a = (2.0 * np.pi / n) * ((k[:, None] * (jnp.arange(hi, dtype=jnp.int32) * lo)[None, :]) % n).astype(F32)
    bb = (2.0 * np.pi / n) * ((k[:, None] * jnp.arange(lo, dtype=jnp.int32)[None, :]) % n).astype(F32)
    ca, sa, cb, sb = jnp.cos(a)[:, :, None], jnp.sin(a)[:, :, None], jnp.cos(bb)[:, None, :], jnp.sin(bb)[:, None, :]
    return (ca * cb - sa * sb).reshape(len(ks), nt), (sa * cb + ca * sb).reshape(len(ks), nt)


def _group_dft_kernel(u0_ref, u1_ref, u2_ref, u3_ref, cs_ref, o_ref, *, groups):
    gd = FNET_GROUP_DIM
    cs = cs_ref[...]
    for g in range(groups):
        cols = slice(g * gd, (g + 1) * gd)
        z = [jnp.dot(u_ref[:, cols], cs, preferred_element_type=F32) for u_ref in (u0_ref, u1_ref, u2_ref, u3_ref)]
        (a0, b0), (a1, b1), (a2, b2), (a3, b3) = [(zq[:, :gd], zq[:, gd:]) for zq in z]
        w_parts = [
            (a0 + a1 + a2 + a3, b0 + b1 + b2 + b3),
            (a0 - b1 - a2 + b3, b0 + a1 - b2 - a3),
            (a0 - a1 + a2 - a3, b0 - b1 + b2 - b3),
            (a0 + b1 - a2 - b3, b0 - a1 - b2 + a3),
        ]
        for r, (re, im) in enumerate(w_parts):
            o_ref[r, 0, :, cols] = re.astype(o_ref.dtype)
            o_ref[r, 1, :, cols] = im.astype(o_ref.dtype)


def _group_dft(p, width):
    b, t, _ = p.shape
    gd = FNET_GROUP_DIM
    c, s = _dft_cos_sin(gd, np.arange(gd), gd)
    cs = (jnp.concatenate([c, s], axis=1) * (gd ** -0.5)).astype(BF16)
    tq = t // DFT_RADIX
    tt = _pick(tq, 256)
    nq = tq // tt

    def rows(q):
        return pl.BlockSpec((None, tt, width), lambda i, j: (i, q * nq + j, 0))

    return pl.pallas_call(
        functools.partial(_group_dft_kernel, groups=width // gd),
        grid=(b, nq),
        in_specs=[rows(q) for q in range(DFT_RADIX)] + [pl.BlockSpec((gd, 2 * gd), lambda i, j: (0, 0))],
        out_specs=pl.BlockSpec((None, DFT_RADIX, 2, tt, width), lambda i, j: (i, 0, 0, j, 0)),
        out_shape=jax.ShapeDtypeStruct((b, DFT_RADIX, 2, tq, width), BF16),
        compiler_params=_params(2),
        name="group_dft",
    )(p, p, p, p, cs)


def _seq_dft_kernel(a_ref, v_ref, perm_ref, o_ref, *, radix, sub, tm):
    perm = perm_ref[...]
    for i in range(a_ref.shape[1] // tm):
        ys = [jnp.dot(a_ref[r, i * tm:(i + 1) * tm, :], v_ref[r], preferred_element_type=F32).astype(BF16)
              for r in range(radix)]
        for g in range(tm // sub):
            stacked = jnp.concatenate([y[g * sub:(g + 1) * sub, :] for y in ys], axis=0)
            row0 = (i * tm + g * sub) * radix
            o_ref[row0:row0 + radix * sub, :] = jnp.dot(
                perm, stacked, preferred_element_type=F32).astype(o_ref.dtype)


def _seq_dft(v):
    b, radix, _, tq, w = v.shape
    t = radix * tq
    mats = []
    for r in range(radix):
        c, s = _dft_cos_sin(t, np.arange(r, t, radix), tq)
        mats.append(jnp.concatenate([c, -s], axis=1) * (t ** -0.5))
    a = jnp.stack(mats).astype(BF16)
    tm = _pick(tq, 512)
    tn = _pick(w, 512)
    sub = min(tm, 2 * LANES // radix)
    rows = np.arange(radix * sub)
    perm = np.zeros((radix * sub, radix * sub), np.float32)
    perm[rows, (rows % radix) * sub + rows // radix] = 1.0
    return pl.pallas_call(
        functools.partial(_seq_dft_kernel, radix=radix, sub=sub, tm=tm),
        grid=(b, w // tn),
        in_specs=[pl.BlockSpec((radix, tq, 2 * tq), lambda g, j: (0, 0, 0), pipeline_mode=pl.Buffered(1)),
                  pl.BlockSpec((None, radix, 2 * tq, tn), lambda g, j: (g, 0, 0, j)),
                  pl.BlockSpec((radix * sub, radix * sub), lambda g, j: (0, 0))],
        out_specs=pl.BlockSpec((None, t, tn), lambda g, j: (g, 0, j)),
        out_shape=jax.ShapeDtypeStruct((b, t, w), BF16),
        compiler_params=_params(2),
        name="seq_dft",
    )(a, v.reshape(b, radix, 2 * tq, w), jnp.asarray(perm, BF16))


def _conv_cols(t, width):
    tc = LANES
    while tc * 2 * t <= CONV_BLOCK_ELEMS and width % (tc * 2) == 0:
        tc *= 2
    return tc


def _dwconv3(z, w_ref):
    t = z.shape[0]
    sub = min(F32_SUBLANES, t)
    row = lax.broadcasted_iota(jnp.int32, (sub, z.shape[1]), 0)
    zp = pltpu.roll(z, 1, axis=0)
    zn = pltpu.roll(z, t - 1, axis=0)
    zp = jnp.concatenate([jnp.where(row == 0, 0.0, zp[:sub]), zp[sub:]], axis=0)
    zn = jnp.concatenate([zn[:t - sub], jnp.where(row == sub - 1, 0.0, zn[t - sub:])], axis=0)
    return zp * w_ref[0:1, :] + z * w_ref[1:2, :] + zn * w_ref[2:3, :]


def _shortconv_kernel(bg_ref, cg_ref, hv_ref, w_ref, o_ref):
    z = cg_ref[...].astype(F32) * hv_ref[...].astype(F32)
    o_ref[...] = (bg_ref[...].astype(F32) * _dwconv3(z, w_ref)).astype(o_ref.dtype)


def _shortconv(p, conv_w, layer, col0, width):
    b, t, _ = p.shape
    tc = _conv_cols(t, math.gcd(width, col0))
    nb = width // tc
    c0 = col0 // tc
    return pl.pallas_call(
        _shortconv_kernel,
        grid=(b, nb),
        in_specs=[pl.BlockSpec((None, t, tc), lambda i, j: (i, 0, c0 + j)),
                  pl.BlockSpec((None, t, tc), lambda i, j: (i, 0, c0 + nb + j)),
                  pl.BlockSpec((None, t, tc), lambda i, j: (i, 0, c0 + 2 * nb + j)),
                  pl.BlockSpec((None, 3, tc), lambda i, j: (layer, 0, j))],
        out_specs=pl.BlockSpec((None, t, tc), lambda i, j: (i, 0, j)),
        out_shape=jax.ShapeDtypeStruct((b, t, width), BF16),
        compiler_params=_params(2),
        name="shortconv_gate",
    )(p, p, p, conv_w)


def _ffn_gate_kernel(g_ref, v_ref, wg_ref, wv_ref, o_ref):
    gate = _dwconv3(g_ref[...].astype(F32), wg_ref)
    val = _dwconv3(v_ref[...].astype(F32), wv_ref)
    o_ref[...] = (gate * val / (1.0 + jnp.exp2(gate * (-LOG2E)))).astype(o_ref.dtype)


def _ffn_gate(u, conv_w, layer):
    b, t, f2 = u.shape
    f = f2 // 2
    tc = _conv_cols(t, f)
    nb = f // tc
    return pl.pallas_call(
        _ffn_gate_kernel,
        grid=(b, nb),
        in_specs=[pl.BlockSpec((None, t, tc), lambda i, j: (i, 0, j)),
                  pl.BlockSpec((None, t, tc), lambda i, j: (i, 0, nb + j)),
                  pl.BlockSpec((None, 3, tc), lambda i, j: (layer, 0, j)),
                  pl.BlockSpec((None, 3, tc), lambda i, j: (layer, 0, nb + j))],
        out_specs=pl.BlockSpec((None, t, tc), lambda i, j: (i, 0, j)),
        out_shape=jax.ShapeDtypeStruct((b, t, f), BF16),
        compiler_params=_params(2),
        name="ffn_conv_gate",
    )(u, u, conv_w, conv_w)


def _proj(parts, w, layer, out_dtype, **kw):
    b, t, _ = parts[0].shape
    y = _matmul([a.reshape(b * t, a.shape[2]) for a in parts], w, layer, out_dtype, **kw)
    return y.reshape(b, t, -1)


def _even_mixer(hl, hc, w_in, rpb, sink, w_out, e, rope, ctx_live):
    na_heads = rpb.shape[1]
    wa_heads = sink.shape[1]
    na_w = na_heads * HEAD_DIM
    wa_w = wa_heads * HEAD_DIM
    kv_w = WA_KV_HEADS * HEAD_DIM
    p = _proj([hl], w_in, e, BF16, name="even_w_in")
    pc = _proj([hc], w_in, e, BF16, name="even_w_in_ctx")
    oa = _neighborhood_attn(p, pc, rpb[e], na_heads)
    ob = _window_gqa(p, pc, sink[e].astype(F32), 3 * na_w, 3 * na_w + wa_w, 3 * na_w + wa_w + kv_w,
                     wa_heads, rope)
    yl = _proj([oa, ob], w_out, e, BF16, name="even_w_out")
    yc = None
    if ctx_live:
        yc = _proj([_ctx_attn(pc, sink[e], na_heads, wa_heads)], w_out, e, BF16, name="even_w_out_ctx")
    return yl, yc


def _odd_mixer(h, w_in, conv_w, w_out, o):
    sc_w = conv_w.shape[2]
    f_w = w_in.shape[2] - 3 * sc_w
    p = _proj([h], w_in, o, BF16, name="odd_w_in")
    f = _seq_dft(_group_dft(p, f_w))
    sc = _shortconv(p, conv_w, o, f_w, sc_w)
    return _proj([f, sc], w_out, o, BF16, name="odd_w_out")


def _conv_ffn(h, w_up, conv_w, w_down, i):
    u = _proj([h], w_up, i, BF16, name="ffn_w_up")
    a = _ffn_gate(u, conv_w, i)
    return _proj([a], w_down, i, BF16, tm=512, tn=1024, name="ffn_w_down")


def kernel(x, c, ctx, c_ctx, w_mod, b_mod, g_mix_pre, g_mix_post, g_ffn_pre, g_ffn_post, even_w_in, even_rpb, even_sink, even_w_out, odd_w_in, odd_conv, odd_w_out, ffn_w_up, ffn_conv, ffn_w_down):
    b, t, d = x.shape
    depth = w_mod.shape[0]
    rows = -(-(b + 1) // 8) * 8
    cond = jnp.zeros((rows, d), F32).at[:b].set(c).at[b].set(c_ctx)
    mod = _modulation(cond, w_mod, b_mod)
    rope = _rope_tables(t)
    xc = ctx
    hl = _prenorm(x, g_mix_pre[0], mod, 0, 0)
    for i in range(depth):
        ctx_live = any(j % 2 == 0 for j in range(i + 1, depth))
        need_hc = (i % 2 == 0) or ctx_live
        hc = _prenorm(xc, g_mix_pre[i], mod, i, 0, fixed_row=b) if need_hc else None
        if i % 2 == 0:
            e = i // 2
            yl, yc = _even_mixer(hl, hc, even_w_in, even_rpb, even_sink, even_w_out, e, rope, ctx_live)
        else:
            o = i // 2
            yl = _odd_mixer(hl, odd_w_in, odd_conv, odd_w_out, o)
            yc = _odd_mixer(hc, odd_w_in, odd_conv, odd_w_out, o) if ctx_live else None
        x, hl = _residual(x, yl, g_mix_post[i], mod, i, 2, nxt=(g_ffn_pre[i], i, 3))
        yf = _conv_ffn(hl, ffn_w_up, ffn_conv, ffn_w_down, i)
        nxt = (g_mix_pre[i + 1], i + 1, 0) if i + 1 < depth else None
        x, hl = _residual(x, yf, g_ffn_post[i], mod, i, 5, nxt=nxt)
        if ctx_live:
            xc, hcf = _residual(xc, yc, g_mix_post[i], mod, i, 2, nxt=(g_ffn_pre[i], i, 3), fixed_row=b)
            yfc = _conv_ffn(hcf, ffn_w_up, ffn_conv, ffn_w_down, i)
            xc, _ = _residual(xc, yfc, g_ffn_post[i], mod, i, 5, fixed_row=b)
    return x
```

```python
import functools
import math

import numpy as np
import jax
import jax.numpy as jnp
from jax import lax
from jax.experimental import pallas as pl
from jax.experimental.pallas import tpu as pltpu

F32 = jnp.float32
BF16 = jnp.bfloat16

GRID_W = 64
HEAD_DIM = 128
NA_KH = 8
NA_KW = 16
WA_KV_HEADS = 4
WA_WINDOW = 128
WA_BLOCK = 128
FNET_GROUP_DIM = 128
ROPE_THETA = 10000.0
RMS_EPS = 1e-6
NEG_INF = -1e30
LOG2E = 1.4426950408889634

V7X_VMEM_BYTES = 64 * 1024 * 1024
VMEM_LIMIT = V7X_VMEM_BYTES - 8 * 1024 * 1024
LANES = 128

NA_ROW_BLOCK = 4
NA_HEADS_PER_STEP = 8
WA_KV_PER_STEP = 4
CTX_HEADS_PER_STEP = 4
CONV_BLOCK_ELEMS = 4096 * 128
F32_SUBLANES = 8
MIN_ROW_TILES_FOR_STAGING = 8


def _params(n_grid):
    return pltpu.CompilerParams(dimension_semantics=("arbitrary",) * n_grid,
                                vmem_limit_bytes=VMEM_LIMIT)


def _pick(n, pref):
    if n <= pref:
        return n
    t = pref
    while n % t:
        t //= 2
    return t


def _mod_kernel(c_ref, w_ref, b_ref, o_ref):
    c = c_ref[...]
    s = (c / (1.0 + jnp.exp(-c))).astype(BF16)
    y = jnp.dot(s, w_ref[...].astype(BF16), preferred_element_type=F32)
    o_ref[...] = y + b_ref[...]


def _modulation(cond, w_mod, b_mod):
    depth, d, n = w_mod.shape
    rows = cond.shape[0]
    tn = _pick(n, 512)
    return pl.pallas_call(
        _mod_kernel,
        grid=(depth, n // tn),
        in_specs=[pl.BlockSpec((rows, d), lambda l, j: (0, 0)),
                  pl.BlockSpec((None, d, tn), lambda l, j: (l, 0, j)),
                  pl.BlockSpec((None, 1, tn), lambda l, j: (l, 0, j))],
        out_specs=pl.BlockSpec((None, rows, tn), lambda l, j: (l, 0, j)),
        out_shape=jax.ShapeDtypeStruct((depth, rows, n), F32),
        compiler_params=_params(2),
        name="adaln_mod",
    )(cond, w_mod, b_mod.reshape(depth, 1, n))


def _rms(v, g):
    return v * lax.rsqrt(jnp.mean(v * v, axis=-1, keepdims=True) + RMS_EPS) * g


def _mod_row(ref, fixed_row):
    row = pl.program_id(0) if fixed_row is None else fixed_row
    return ref[pl.ds(row, 1), :]


def _prenorm_kernel(x_ref, g_ref, sh_ref, sc_ref, h_ref, *, fixed_row):
    sh = _mod_row(sh_ref, fixed_row)
    sc = _mod_row(sc_ref, fixed_row)
    h_ref[...] = (_rms(x_ref[...], g_ref[...]) * (1.0 + sc) + sh).astype(h_ref.dtype)


def _prenorm(x, g, mod, layer, slot, fixed_row=None):
    b, t, d = x.shape
    rows = mod.shape[1]
    tr = _pick(t, 256)
    return pl.pallas_call(
        functools.partial(_prenorm_kernel, fixed_row=fixed_row),
        grid=(b, t // tr),
        in_specs=[pl.BlockSpec((None, tr, d), lambda i, j: (i, j, 0)),
                  pl.BlockSpec((1, d), lambda i, j: (0, 0)),
                  pl.BlockSpec((None, rows, d), lambda i, j: (layer, 0, slot)),
                  pl.BlockSpec((None, rows, d), lambda i, j: (layer, 0, slot + 1))],
        out_specs=pl.BlockSpec((None, tr, d), lambda i, j: (i, j, 0)),
        out_shape=jax.ShapeDtypeStruct((b, t, d), BF16),
        compiler_params=_params(2),
        name="prenorm",
    )(x, g.reshape(1, d), mod, mod)


def _resid_kernel(x_ref, y_ref, gp_ref, gt_ref, *rest, fixed_row, with_next):
    gt = _mod_row(gt_ref, fixed_row)
    xn = x_ref[...] + gt * _rms(y_ref[...].astype(F32), gp_ref[...])
    if with_next:
        gn_ref, sh_ref, sc_ref, xo_ref, h_ref = rest
        sh = _mod_row(sh_ref, fixed_row)
        sc = _mod_row(sc_ref, fixed_row)
        h_ref[...] = (_rms(xn, gn_ref[...]) * (1.0 + sc) + sh).astype(h_ref.dtype)
    else:
        (xo_ref,) = rest
    xo_ref[...] = xn


def _residual(x, y, g_post, mod, layer, gate_slot, nxt=None, fixed_row=None):
    b, t, d = x.shape
    rows = mod.shape[1]
    tr = _pick(t, 256)
    row_spec = pl.BlockSpec((None, tr, d), lambda i, j: (i, j, 0))
    vec_spec = pl.BlockSpec((1, d), lambda i, j: (0, 0))

    def mod_spec(l, s):
        return pl.BlockSpec((None, rows, d), lambda i, j: (l, 0, s))

    in_specs = [row_spec, row_spec, vec_spec, mod_spec(layer, gate_slot)]
    args = [x, y, g_post.reshape(1, d), mod]
    out_specs = [row_spec]
    out_shape = [jax.ShapeDtypeStruct((b, t, d), F32)]
    if nxt is not None:
        g_pre, nl, ns = nxt
        in_specs += [vec_spec, mod_spec(nl, ns), mod_spec(nl, ns + 1)]
        args += [g_pre.reshape(1, d), mod, mod]
        out_specs.append(row_spec)
        out_shape.append(jax.ShapeDtypeStruct((b, t, d), BF16))
    out = pl.pallas_call(
        functools.partial(_resid_kernel, fixed_row=fixed_row, with_next=nxt is not None),
        grid=(b, t // tr),
        in_specs=in_specs, out_specs=out_specs, out_shape=out_shape,
        compiler_params=_params(2),
        name="residual_norm",
    )(*args)
    return (out[0], out[1]) if nxt is not None else (out[0], None)


def _dot_parts(a_refs, w_ref, o_ref):
    acc = None
    k0 = 0
    for a_ref in a_refs:
        kp = a_ref.shape[1]
        part = jnp.dot(a_ref[...], w_ref[k0:k0 + kp, :], preferred_element_type=F32)
        acc = part if acc is None else acc + part
        k0 += kp
    o_ref[...] = acc.astype(o_ref.dtype)


def _mm_cast_kernel(*refs, n_parts):
    a_refs = refs[:n_parts]
    w_ref, o_ref, wbf = refs[n_parts:]

    @pl.when(pl.program_id(1) == 0)
    def _():
        wbf[...] = w_ref[...].astype(BF16)

    _dot_parts(a_refs, wbf, o_ref)


def _mm_staged_kernel(*refs, n_parts, layer, tn):
    a_refs = refs[:n_parts]
    w_hbm, o_ref, stage, wbf, sem = refs[n_parts:]
    j = pl.program_id(0)

    def tile_copy(jj):
        return pltpu.make_async_copy(w_hbm.at[layer, :, pl.ds(pl.multiple_of(jj * tn, tn), tn)], stage, sem)

    @pl.when(pl.program_id(1) == 0)
    def _():
        @pl.when(j == 0)
        def _():
            tile_copy(0).start()

        tile_copy(j).wait()
        wbf[...] = stage[...].astype(BF16)

        @pl.when(j + 1 < pl.num_programs(0))
        def _():
            tile_copy(j + 1).start()

    _dot_parts(a_refs, wbf, o_ref)


def _matmul(a_parts, w, layer, out_dtype, *, tm=1024, tn=1024, name="matmul"):
    m = a_parts[0].shape[0]
    _, k, n = w.shape
    assert sum(a.shape[1] for a in a_parts) == k
    tm = _pick(m, tm)
    n_parts = len(a_parts)
    a_specs = [pl.BlockSpec((tm, a.shape[1]), lambda j, i: (i, 0)) for a in a_parts]
    out_shape = jax.ShapeDtypeStruct((m, n), out_dtype)
    if m // tm >= MIN_ROW_TILES_FOR_STAGING:
        tn = _pick(n, tn)
        return pl.pallas_call(
            functools.partial(_mm_staged_kernel, n_parts=n_parts, layer=layer, tn=tn),
            grid=(n // tn, m // tm),
            in_specs=a_specs + [pl.BlockSpec(memory_space=pl.ANY)],
            out_specs=pl.BlockSpec((tm, tn), lambda j, i: (i, j)),
            out_shape=out_shape,
            scratch_shapes=[pltpu.VMEM((k, tn), F32), pltpu.VMEM((k, tn), BF16), pltpu.SemaphoreType.DMA(())],
            compiler_params=_params(2),
            name=name,
        )(*a_parts, w)
    tn = _pick(n, tn // 2)
    return pl.pallas_call(
        functools.partial(_mm_cast_kernel, n_parts=n_parts),
        grid=(n // tn, m // tm),
        in_specs=a_specs + [pl.BlockSpec((None, k, tn), lambda j, i: (layer, 0, j))],
        out_specs=pl.BlockSpec((tm, tn), lambda j, i: (i, j)),
        out_shape=out_shape,
        scratch_shapes=[pltpu.VMEM((k, tn), BF16)],
        compiler_params=_params(2),
        name=name,
    )(*a_parts, w)


def _na_geometry(rows, rb):
    kh = min(NA_KH, rows)
    kr = min(rb + kh, rows)
    nblk = rows // rb
    slab = np.clip(np.arange(nblk) * rb - kh // 2, 0, rows - kr)
    variants, variant_of = [], []
    for j in range(nblk):
        geo = []
        for a in range(rb):
            r = j * rb + a
            rs = int(np.clip(r - kh // 2, 0, rows - kh))
            lo = rs - int(slab[j])
            geo.append((lo, lo + kh, rs - r + NA_KH - 1))
        geo = tuple(geo)
        if geo not in variants:
            variants.append(geo)
        variant_of.append(variants.index(geo))
    return kr, slab, np.asarray(variant_of), variants


def _na_bias_table(rpb):
    h = rpb.shape[0]
    qcol = np.arange(GRID_W)
    cstart = np.clip(qcol - NA_KW // 2, 0, GRID_W - NA_KW)
    kcol = np.arange(GRID_W)
    col_ok = (kcol[None, :] >= cstart[:, None]) & (kcol[None, :] < cstart[:, None] + NA_KW)
    ext = GRID_W - NA_KW
    padded = jnp.pad(rpb.astype(F32), ((0, 0), (0, 0), (ext, ext)))
    e = jnp.stack([padded[:, :, GRID_W - 1 - qc:2 * GRID_W - 1 - qc] for qc in range(GRID_W)], axis=1)
    e = jnp.where(jnp.asarray(col_ok)[None, :, None, :], e, NEG_INF)
    e = e.reshape(h, GRID_W, (2 * NA_KH - 1) * GRID_W)
    return jnp.pad(e, ((0, 0), (0, 0), (0, GRID_W)), constant_values=NEG_INF)


def _nt_dot(a, b):
    return lax.dot_general(a, b, (((1,), (1,)), ((), ())), preferred_element_type=F32)


def _na_kernel(slab_ref, var_ref, q_ref, k_ref, v_ref, kc_ref, vc_ref, e_ref, o_ref, bias_ref,
               *, kr, hp, variants):
    j = pl.program_id(2)
    gw = GRID_W
    ks = pl.multiple_of(slab_ref[j] * gw, gw)
    nk = kr * gw
    hd = HEAD_DIM
    scale = hd ** -0.5

    changed = (j == 0) | (var_ref[j] != var_ref[jnp.maximum(j - 1, 0)])
    for vi, geo in enumerate(variants):
        @pl.when(changed & (var_ref[j] == vi))
        def _(geo=geo):
            for hh in range(hp):
                for a, (lo, hi, rr0) in enumerate(geo):
                    qrows = slice(a * gw, (a + 1) * gw)
                    if lo > 0:
                        bias_ref[hh, qrows, 0:lo * gw] = jnp.full((gw, lo * gw), NEG_INF, F32)
                    bias_ref[hh, qrows, lo * gw:hi * gw] = e_ref[hh, :, rr0 * gw:(rr0 + hi - lo) * gw]
                    if hi < kr:
                        bias_ref[hh, qrows, hi * gw:] = jnp.full((gw, (kr - hi) * gw), NEG_INF, F32)

    for hh in range(hp):
        cols = slice(hh * hd, (hh + 1) * hd)
        q = q_ref[:, cols]
        nc = kc_ref.shape[0]
        k_all = jnp.concatenate([k_ref[pl.ds(ks, nk), cols], kc_ref[:, cols]], axis=0)
        s_all = _nt_dot(q, k_all) * scale
        s_all = jnp.concatenate([s_all[:, :nk] + bias_ref[hh], s_all[:, nk:]], axis=1)
        p_all = jnp.exp(s_all - jnp.max(s_all, axis=-1, keepdims=True))
        v_all = jnp.concatenate([v_ref[pl.ds(ks, nk), cols], vc_ref[:, cols]], axis=0)
        v_ext = jnp.concatenate([v_all, jnp.ones((nk + nc, hd), BF16)], axis=1)
        oe = jnp.dot(p_all.astype(BF16), v_ext, preferred_element_type=F32)
        o_ref[:, cols] = (oe[:, :hd] / oe[:, hd:hd + 1]).astype(o_ref.dtype)


def _neighborhood_attn(p, pc, rpb, n_heads):
    b, t, _ = p.shape
    l = pc.shape[1]
    rows = t // GRID_W
    rb = min(NA_ROW_BLOCK, rows)
    hp = min(NA_HEADS_PER_STEP, n_heads)
    assert n_heads % hp == 0
    ng = n_heads // hp
    kr, slab, variant_of, variants = _na_geometry(rows, rb)
    table = _na_bias_table(rpb)
    nblk = rows // rb
    tq = rb * GRID_W
    w = hp * HEAD_DIM
    grid_spec = pltpu.PrefetchScalarGridSpec(
        num_scalar_prefetch=2,
        grid=(b, ng, nblk),
        in_specs=[
            pl.BlockSpec((None, tq, w), lambda i, h, j, s, v: (i, j, h)),
            pl.BlockSpec((None, t, w), lambda i, h, j, s, v: (i, 0, ng + h)),
            pl.BlockSpec((None, t, w), lambda i, h, j, s, v: (i, 0, 2 * ng + h)),
            pl.BlockSpec((None, l, w), lambda i, h, j, s, v: (i, 0, ng + h)),
            pl.BlockSpec((None, l, w), lambda i, h, j, s, v: (i, 0, 2 * ng + h)),
            pl.BlockSpec((hp, GRID_W, table.shape[2]), lambda i, h, j, s, v: (h, 0, 0)),
        ],
        out_specs=pl.BlockSpec((None, tq, w), lambda i, h, j, s, v: (i, j, h)),
        scratch_shapes=[pltpu.VMEM((hp, tq, kr * GRID_W), F32)],
    )
    return pl.pallas_call(
        functools.partial(_na_kernel, kr=kr, hp=hp, variants=tuple(variants)),
        grid_spec=grid_spec,
        out_shape=jax.ShapeDtypeStruct((b, t, n_heads * HEAD_DIM), BF16),
        compiler_params=_params(3),
        name="neighborhood_attn",
    )(jnp.asarray(slab, jnp.int32), jnp.asarray(variant_of, jnp.int32), p, p, p, pc, pc, table)


def _rope_tables(s):
    tpos = jnp.arange(s)
    row = (tpos // GRID_W).astype(F32)
    col = (tpos % GRID_W).astype(F32)
    nf = HEAD_DIM // 4
    inv = ROPE_THETA ** (-jnp.arange(nf, dtype=F32) / nf)
    lane = np.arange(HEAD_DIM)
    axis = lane // (2 * nf)
    second = (lane % (2 * nf)) >= nf
    freq = lane % nf
    pos = jnp.where(jnp.asarray(axis)[None, :] == 0, row[:, None], col[:, None])
    ang = pos * inv[jnp.asarray(freq)][None, :]
    cos, sin = jnp.cos(ang), jnp.sin(ang)
    sec = jnp.asarray(second)[None, :]
    return cos, jnp.where(sec, 0.0, -sin), jnp.where(sec, sin, 0.0)


def _rope(x, c, s1, s2):
    nf = HEAD_DIM // 4
    return x * c + pltpu.roll(x, HEAD_DIM - nf, axis=1) * s1 + pltpu.roll(x, nf, axis=1) * s2


def _wa_kernel(sink_ref, q_ref, k_ref, v_ref, kc_ref, vc_ref, c_ref, s1_ref, s2_ref, o_ref,
               *, group, kp, seq):
    kg = pl.program_id(1)
    n = pl.program_id(2)
    wb, hd = WA_BLOCK, HEAD_DIM
    nl = 3 * wb
    q0 = pl.multiple_of(n * wb, wb)
    bs = pl.multiple_of(jnp.clip(n * wb - wb, 0, seq - nl), wb)
    scale = hd ** -0.5
    cq, s1q, s2q = c_ref[pl.ds(q0, wb), :], s1_ref[pl.ds(q0, wb), :], s2_ref[pl.ds(q0, wb), :]
    ck, s1k, s2k = c_ref[pl.ds(bs, nl), :], s1_ref[pl.ds(bs, nl), :], s2_ref[pl.ds(bs, nl), :]
    rowi = lax.broadcasted_iota(jnp.int32, (group * wb, nl), 0)
    kpos = bs + lax.broadcasted_iota(jnp.int32, (group * wb, nl), 1)
    qpos = q0 + (rowi & (wb - 1))
    in_window = jnp.abs(kpos - qpos) <= WA_WINDOW
    gi = lax.broadcasted_iota(jnp.int32, (group * wb, 1), 0) // wb
    for kk in range(kp):
        kcols = slice(kk * hd, (kk + 1) * hd)
        q = jnp.concatenate(
            [_rope(q_ref[:, (kk * group + g) * hd:(kk * group + g + 1) * hd].astype(F32), cq, s1q, s2q)
             .astype(BF16) for g in range(group)], axis=0)
        kb = _rope(k_ref[pl.ds(bs, nl), kcols].astype(F32), ck, s1k, s2k).astype(BF16)
        s_loc = jnp.where(in_window, _nt_dot(q, kb) * scale, NEG_INF)
        s_ctx = _nt_dot(q, kc_ref[:, kcols]) * scale
        sink = jnp.zeros((group * wb, 1), F32)
        for g in range(group):
            sink = jnp.where(gi == g, sink_ref[(kg * kp + kk) * group + g], sink)
        m = jnp.maximum(jnp.maximum(jnp.max(s_loc, axis=-1, keepdims=True),
                                    jnp.max(s_ctx, axis=-1, keepdims=True)), sink)
        p_loc = jnp.exp(s_loc - m)
        p_ctx = jnp.exp(s_ctx - m)
        den = (jnp.sum(p_loc, axis=-1, keepdims=True) + jnp.sum(p_ctx, axis=-1, keepdims=True)
               + jnp.exp(sink - m))
        o = (jnp.dot(p_loc.astype(BF16), v_ref[pl.ds(bs, nl), kcols], preferred_element_type=F32)
             + jnp.dot(p_ctx.astype(BF16), vc_ref[:, kcols], preferred_element_type=F32))
        o = (o / den).astype(o_ref.dtype)
        for g in range(group):
            o_ref[:, (kk * group + g) * hd:(kk * group + g + 1) * hd] = o[g * wb:(g + 1) * wb, :]


def _window_gqa(p, pc, sink, q_col, k_col, v_col, n_heads, rope):
    b, t, _ = p.shape
    l = pc.shape[1]
    kvh = WA_KV_HEADS
    group = n_heads // kvh
    kp = min(WA_KV_PER_STEP, kvh)
    hd, wb = HEAD_DIM, WA_BLOCK
    qw = kp * group * hd
    kw = kp * hd
    assert kvh % kp == 0 and q_col % qw == 0 and k_col % kw == 0 and v_col % kw == 0
    assert t % wb == 0 and t >= 3 * wb
    cos, s1, s2 = rope
    tab_spec = pl.BlockSpec((t, hd), lambda i, k, n: (0, 0))
    return pl.pallas_call(
        functools.partial(_wa_kernel, group=group, kp=kp, seq=t),
        grid=(b, kvh // kp, t // wb),
        in_specs=[
            pl.BlockSpec(memory_space=pltpu.SMEM),
            pl.BlockSpec((None, wb, qw), lambda i, k, n: (i, n, q_col // qw + k)),
            pl.BlockSpec((None, t, kw), lambda i, k, n: (i, 0, k_col // kw + k)),
            pl.BlockSpec((None, t, kw), lambda i, k, n: (i, 0, v_col // kw + k)),
            pl.BlockSpec((None, l, kw), lambda i, k, n: (i, 0, k_col // kw + k)),
            pl.BlockSpec((None, l, kw), lambda i, k, n: (i, 0, v_col // kw + k)),
            tab_spec, tab_spec, tab_spec,
        ],
        out_specs=pl.BlockSpec((None, wb, qw), lambda i, k, n: (i, n, k)),
        out_shape=jax.ShapeDtypeStruct((b, t, n_heads * hd), BF16),
        compiler_params=_params(3),
        name="window_gqa",
    )(sink, p, p, p, pc, pc, cos, s1, s2)


def _ctx_kernel(sink_ref, q_ref, *refs, hp):
    k_refs, v_refs, o_ref = refs[:hp], refs[hp:2 * hp], refs[2 * hp]
    hd = HEAD_DIM
    scale = hd ** -0.5
    for hh in range(hp):
        cols = slice(hh * hd, (hh + 1) * hd)
        s = _nt_dot(q_ref[:, cols], k_refs[hh][...]) * scale
        sink = sink_ref[pl.program_id(1) * hp + hh]
        m = jnp.maximum(jnp.max(s, axis=-1, keepdims=True), sink)
        p = jnp.exp(s - m)
        den = jnp.sum(p, axis=-1, keepdims=True) + jnp.exp(sink - m)
        o = jnp.dot(p.astype(BF16), v_refs[hh][...], preferred_element_type=F32)
        o_ref[:, cols] = (o / den).astype(o_ref.dtype)


def _ctx_attn(pc, sink, na_heads, wa_heads):
    b, l, _ = pc.shape
    hd = HEAD_DIM
    group = wa_heads // WA_KV_HEADS
    hp = CTX_HEADS_PER_STEP
    while na_heads % hp or wa_heads % hp:
        hp //= 2
    qb0 = 3 * na_heads
    kb0 = qb0 + wa_heads
    vb0 = kb0 + WA_KV_HEADS
    sink_all = jnp.concatenate([jnp.full((na_heads,), NEG_INF, F32), sink.astype(F32)])

    def qmap(i, g):
        return (i, 0, jnp.where(g * hp < na_heads, g, (qb0 - na_heads) // hp + g))

    def kv_spec(hh, mha0, gqa0):
        def index(i, g):
            h = g * hp + hh
            return (i, 0, jnp.where(h < na_heads, mha0 + h, gqa0 + (h - na_heads) // group))
        return pl.BlockSpec((None, l, hd), index)

    return pl.pallas_call(
        functools.partial(_ctx_kernel, hp=hp),
        grid=(b, (na_heads + wa_heads) // hp),
        in_specs=[pl.BlockSpec(memory_space=pltpu.SMEM), pl.BlockSpec((None, l, hp * hd), qmap)]
        + [kv_spec(hh, na_heads, kb0) for hh in range(hp)]
        + [kv_spec(hh, 2 * na_heads, vb0) for hh in range(hp)],
        out_specs=pl.BlockSpec((None, l, hp * hd), lambda i, g: (i, 0, g)),
        out_shape=jax.ShapeDtypeStruct((b, l, (na_heads + wa_heads) * hd), BF16),
        compiler_params=_params(2),
        name="context_attn",
    )(sink_all, pc, *([pc] * (2 * hp)))


DFT_RADIX = 4
def _dft_cos_sin(n, ks, nt):
    lo = 1
    while lo * lo < nt:
        lo *= 2
    hi = nt // lo
    k = jnp.asarray(ks, dtype=jnp.int32)
    a = (2.0 * np.pi / n) * ((k[:, None] * (jnp.arange(hi, dtype=jnp.int32) * lo)[None, :]) % n).astype(F32)
    bb = (2.0 * np.pi / n) * ((k[:, None] * jnp.arange(lo, dtype=jnp.int32)[None, :]) % n).astype(F32)
    ca, sa, cb, sb = jnp.cos(a)[:, :, None], jnp.sin(a)[:, :, None], jnp.cos(bb)[:, None, :], jnp.sin(bb)[:, None, :]
    return (ca * cb - sa * sb).reshape(len(ks), nt), (sa * cb + ca * sb).reshape(len(ks), nt)


def _group_dft_kernel(u0_ref, u1_ref, u2_ref, u3_ref, cs_ref, o_ref, *, groups):
    gd = FNET_GROUP_DIM
    cs = cs_ref[...]
    for g in range(groups):
        cols = slice(g * gd, (g + 1) * gd)
        z = [jnp.dot(u_ref[:, cols], cs, preferred_element_type=F32) for u_ref in (u0_ref, u1_ref, u2_ref, u3_ref)]
        (a0, b0), (a1, b1), (a2, b2), (a3, b3) = [(zq[:, :gd], zq[:, gd:]) for zq in z]
        w_parts = [
            (a0 + a1 + a2 + a3, b0 + b1 + b2 + b3),
            (a0 - b1 - a2 + b3, b0 + a1 - b2 - a3),
            (a0 - a1 + a2 - a3, b0 - b1 + b2 - b3),
            (a0 + b1 - a2 - b3, b0 - a1 - b2 + a3),
        ]
        for r, (re, im) in enumerate(w_parts):
            o_ref[r, 0, :, cols] = re.astype(o_ref.dtype)
            o_ref[r, 1, :, cols] = im.astype(o_ref.dtype)


def _group_dft(p, width):
    b, t, _ = p.shape
    gd = FNET_GROUP_DIM
    c, s = _dft_cos_sin(gd, np.arange(gd), gd)
    cs = (jnp.concatenate([c, s], axis=1) * (gd ** -0.5)).astype(BF16)
    tq = t // DFT_RADIX
    tt = _pick(tq, 256)
    nq = tq // tt

    def rows(q):
        return pl.BlockSpec((None, tt, width), lambda i, j: (i, q * nq + j, 0))

    return pl.pallas_call(
        functools.partial(_group_dft_kernel, groups=width // gd),
        grid=(b, nq),
        in_specs=[rows(q) for q in range(DFT_RADIX)] + [pl.BlockSpec((gd, 2 * gd), lambda i, j: (0, 0))],
        out_specs=pl.BlockSpec((None, DFT_RADIX, 2, tt, width), lambda i, j: (i, 0, 0, j, 0)),
        out_shape=jax.ShapeDtypeStruct((b, DFT_RADIX, 2, tq, width), BF16),
        compiler_params=_params(2),
        name="group_dft",
    )(p, p, p, p, cs)


def _seq_dft_kernel(a_ref, v_ref, perm_ref, o_ref, *, radix, sub, tm):
    perm = perm_ref[...]
    for i in range(a_ref.shape[1] // tm):
        ys = [jnp.dot(a_ref[r, i * tm:(i + 1) * tm, :], v_ref[r], preferred_element_type=F32).astype(BF16)
              for r in range(radix)]
        for g in range(tm // sub):
            stacked = jnp.concatenate([y[g * sub:(g + 1) * sub, :] for y in ys], axis=0)
            row0 = (i * tm + g * sub) * radix
            o_ref[row0:row0 + radix * sub, :] = jnp.dot(
                perm, stacked, preferred_element_type=F32).astype(o_ref.dtype)


def _seq_dft(v):
    b, radix, _, tq, w = v.shape
    t = radix * tq
    mats = []
    for r in range(radix):
        c, s = _dft_cos_sin(t, np.arange(r, t, radix), tq)
        mats.append(jnp.concatenate([c, -s], axis=1) * (t ** -0.5))
    a = jnp.stack(mats).astype(BF16)
    tm = _pick(tq, 512)
    tn = _pick(w, 512)
    sub = min(tm, 2 * LANES // radix)
    rows = np.arange(radix * sub)
    perm = np.zeros((radix * sub, radix * sub), np.float32)
    perm[rows, (rows % radix) * sub + rows // radix] = 1.0
    return pl.pallas_call(
        functools.partial(_seq_dft_kernel, radix=radix, sub=sub, tm=tm),
        grid=(b, w // tn),
        in_specs=[pl.BlockSpec((radix, tq, 2 * tq), lambda g, j: (0, 0, 0), pipeline_mode=pl.Buffered(1)),
                  pl.BlockSpec((None, radix, 2 * tq, tn), lambda g, j: (g, 0, 0, j)),
                  pl.BlockSpec((radix * sub, radix * sub), lambda g, j: (0, 0))],
        out_specs=pl.BlockSpec((None, t, tn), lambda g, j: (g, 0, j)),
        out_shape=jax.ShapeDtypeStruct((b, t, w), BF16),
        compiler_params=_params(2),
        name="seq_dft",
    )(a, v.reshape(b, radix, 2 * tq, w), jnp.asarray(perm, BF16))


def _conv_cols(t, width):
    tc = LANES
    while tc * 2 * t <= CONV_BLOCK_ELEMS and width % (tc * 2) == 0:
        tc *= 2
    return tc


def _dwconv3(z, w_ref):
    t = z.shape[0]
    sub = min(F32_SUBLANES, t)
    row = lax.broadcasted_iota(jnp.int32, (sub, z.shape[1]), 0)
    zp = pltpu.roll(z, 1, axis=0)
    zn = pltpu.roll(z, t - 1, axis=0)
    zp = jnp.concatenate([jnp.where(row == 0, 0.0, zp[:sub]), zp[sub:]], axis=0)
    zn = jnp.concatenate([zn[:t - sub], jnp.where(row == sub - 1, 0.0, zn[t - sub:])], axis=0)
    return zp * w_ref[0:1, :] + z * w_ref[1:2, :] + zn * w_ref[2:3, :]


def _shortconv_kernel(bg_ref, cg_ref, hv_ref, w_ref, o_ref):
    z = cg_ref[...].astype(F32) * hv_ref[...].astype(F32)
    o_ref[...] = (bg_ref[...].astype(F32) * _dwconv3(z, w_ref)).astype(o_ref.dtype)


def _shortconv(p, conv_w, layer, col0, width):
    b, t, _ = p.shape
    tc = _conv_cols(t, math.gcd(width, col0))
    nb = width // tc
    c0 = col0 // tc
    return pl.pallas_call(
        _shortconv_kernel,
        grid=(b, nb),
        in_specs=[pl.BlockSpec((None, t, tc), lambda i, j: (i, 0, c0 + j)),
                  pl.BlockSpec((None, t, tc), lambda i, j: (i, 0, c0 + nb + j)),
                  pl.BlockSpec((None, t, tc), lambda i, j: (i, 0, c0 + 2 * nb + j)),
                  pl.BlockSpec((None, 3, tc), lambda i, j: (layer, 0, j))],
        out_specs=pl.BlockSpec((None, t, tc), lambda i, j: (i, 0, j)),
        out_shape=jax.ShapeDtypeStruct((b, t, width), BF16),
        compiler_params=_params(2),
        name="shortconv_gate",
    )(p, p, p, conv_w)


def _ffn_gate_kernel(g_ref, v_ref, wg_ref, wv_ref, o_ref):
    gate = _dwconv3(g_ref[...].astype(F32), wg_ref)
    val = _dwconv3(v_ref[...].astype(F32), wv_ref)
    o_ref[...] = (gate * val / (1.0 + jnp.exp2(gate * (-LOG2E)))).astype(o_ref.dtype)


def _ffn_gate(u, conv_w, layer):
    b, t, f2 = u.shape
    f = f2 // 2
    tc = _conv_cols(t, f)
    nb = f // tc
    return pl.pallas_call(
        _ffn_gate_kernel,
        grid=(b, nb),
        in_specs=[pl.BlockSpec((None, t, tc), lambda i, j: (i, 0, j)),
                  pl.BlockSpec((None, t, tc), lambda i, j: (i, 0, nb + j)),
                  pl.BlockSpec((None, 3, tc), lambda i, j: (layer, 0, j)),
                  pl.BlockSpec((None, 3, tc), lambda i, j: (layer, 0, nb + j))],
        out_specs=pl.BlockSpec((None, t, tc), lambda i, j: (i, 0, j)),
        out_shape=jax.ShapeDtypeStruct((b, t, f), BF16),
        compiler_params=_params(2),
        name="ffn_conv_gate",
    )(u, u, conv_w, conv_w)


def _proj(parts, w, layer, out_dtype, **kw):
    b, t, _ = parts[0].shape
    y = _matmul([a.reshape(b * t, a.shape[2]) for a in parts], w, layer, out_dtype, **kw)
    return y.reshape(b, t, -1)


def _even_mixer(hl, hc, w_in, rpb, sink, w_out, e, rope, ctx_live):
    na_heads = rpb.shape[1]
    wa_heads = sink.shape[1]
    na_w = na_heads * HEAD_DIM
    wa_w = wa_heads * HEAD_DIM
    kv_w = WA_KV_HEADS * HEAD_DIM
    p = _proj([hl], w_in, e, BF16, name="even_w_in")
    pc = _proj([hc], w_in, e, BF16, name="even_w_in_ctx")
    oa = _neighborhood_attn(p, pc, rpb[e], na_heads)
    ob = _window_gqa(p, pc, sink[e].astype(F32), 3 * na_w, 3 * na_w + wa_w, 3 * na_w + wa_w + kv_w,
                     wa_heads, rope)
    yl = _proj([oa, ob], w_out, e, BF16, name="even_w_out")
    yc = None
    if ctx_live:
        yc = _proj([_ctx_attn(pc, sink[e], na_heads, wa_heads)], w_out, e, BF16, name="even_w_out_ctx")
    return yl, yc


def _odd_mixer(h, w_in, conv_w, w_out, o):
    sc_w = conv_w.shape[2]
    f_w = w_in.shape[2] - 3 * sc_w
    p = _proj([h], w_in, o, BF16, name="odd_w_in")
    f = _seq_dft(_group_dft(p, f_w))
    sc = _shortconv(p, conv_w, o, f_w, sc_w)
    return _proj([f, sc], w_out, o, BF16, name="odd_w_out")


def _conv_ffn(h, w_up, conv_w, w_down, i):
    u = _proj([h], w_up, i, BF16, name="ffn_w_up")
    a = _ffn_gate(u, conv_w, i)
    return _proj([a], w_down, i, BF16, tm=512, tn=1024, name="ffn_w_down")


def kernel(x, c, ctx, c_ctx, w_mod, b_mod, g_mix_pre, g_mix_post, g_ffn_pre, g_ffn_post, even_w_in, even_rpb, even_sink, even_w_out, odd_w_in, odd_conv, odd_w_out, ffn_w_up, ffn_conv, ffn_w_down):
    b, t, d = x.shape
    depth = w_mod.shape[0]
    rows = -(-(b + 1) // 8) * 8
    cond = jnp.zeros((rows, d), F32).at[:b].set(c).at[b].set(c_ctx)
    mod = _modulation(cond, w_mod, b_mod)
    rope = _rope_tables(t)
    xc = ctx
    hl = _prenorm(x, g_mix_pre[0], mod, 0, 0)
    for i in range(depth):
        ctx_live = any(j % 2 == 0 for j in range(i + 1, depth))
        need_hc = (i % 2 == 0) or ctx_live
        hc = _prenorm(xc, g_mix_pre[i], mod, i, 0, fixed_row=b) if need_hc else None
        if i % 2 == 0:
            e = i // 2
            yl, yc = _even_mixer(hl, hc, even_w_in, even_rpb, even_sink, even_w_out, e, rope, ctx_live)
        else:
            o = i // 2
            yl = _odd_mixer(hl, odd_w_in, odd_conv, odd_w_out, o)
            yc = _odd_mixer(hc, odd_w_in, odd_conv, odd_w_out, o) if ctx_live else None
        x, hl = _residual(x, yl, g_mix_post[i], mod, i, 2, nxt=(g_ffn_pre[i], i, 3))
        yf = _conv_ffn(hl, ffn_w_up, ffn_conv, ffn_w_down, i)
        nxt = (g_mix_pre[i + 1], i + 1, 0) if i + 1 < depth else None
        x, hl = _residual(x, yf, g_ffn_post[i], mod, i, 5, nxt=nxt)
        if ctx_live:
            xc, hcf = _residual(xc, yc, g_mix_post[i], mod, i, 2, nxt=(g_ffn_pre[i], i, 3), fixed_row=b)
            yfc = _conv_ffn(hcf, ffn_w_up, ffn_conv, ffn_w_down, i)
            xc, _ = _residual(xc, yfc, g_ffn_post[i], mod, i, 5, fixed_row=b)
    return x
```
